```python
import jax, jax.numpy as jnp
from jax import lax
import numpy as np

D_MODEL = 1024
BATCH = 32
SEQ = 2048
DEPTH = 1

N_MEM = 256
MIX_WIDTH = D_MODEL
POOL_WIDTH = MIX_WIDTH // 2
POOL_GROUPS = 4
POOL_GROUP_DIM = POOL_WIDTH // POOL_GROUPS
POOL_WINDOWS = (2, 4, 8, 16)
GLA_VALUE_WIDTH = MIX_WIDTH - POOL_WIDTH
GLA_KEY_WIDTH = GLA_VALUE_WIDTH // 2
GLA_HEADS = 4
GLA_DK = GLA_KEY_WIDTH // GLA_HEADS
GLA_DV = GLA_VALUE_WIDTH // GLA_HEADS
GLA_GATE_RANK = 16
GLA_GATE_NORMALIZER = 16.0
GLA_CHUNK = 64
OFF_Q = POOL_WIDTH
OFF_K = OFF_Q + GLA_KEY_WIDTH
OFF_V = OFF_K + GLA_KEY_WIDTH
OFF_G = OFF_V + GLA_VALUE_WIDTH
OFF_R = OFF_G + GLA_GATE_RANK
IN_PROJ_WIDTH = OFF_R + GLA_VALUE_WIDTH
XATTN_HEADS = 4
XATTN_HEAD_DIM = D_MODEL // XATTN_HEADS
D_FF = 2816
CONV_WIDTH = 3
EPS = 1e-6

kernel_name = 'hybrid_pool_gla_memxattn_convffn'


def rms_norm(x, w):
    x32 = x.astype(jnp.float32)
    y = x32 * lax.rsqrt(jnp.mean(x32 * x32, axis=-1, keepdims=True) + EPS)
    return (y * w.astype(jnp.float32)).astype(x.dtype)


def multiscale_pool(p, pool_w, pool_scale):
    b, s, _ = p.shape
    pg = p.reshape(b, s, POOL_GROUPS, POOL_GROUP_DIM)
    cs = jnp.cumsum(pg.astype(jnp.float32), axis=1)
    t = jnp.arange(s)
    pooled = []
    for g, w in enumerate(POOL_WINDOWS):
        c = cs[:, :, g]
        c_prev = jnp.pad(c, ((0, 0), (w, 0), (0, 0)))[:, :s]
        cnt = jnp.minimum(t + 1, w).astype(jnp.float32)[None, :, None]
        pooled.append((c - c_prev) / cnt)
    pooled = jnp.stack(pooled, axis=2).astype(p.dtype) - pg
    mixed = jnp.einsum('bsgc,gcd->bsgd', pooled, pool_w)
    return mixed.reshape(b, s, POOL_WIDTH) * pool_scale


def gla_chunked(q, k, v, log_g):
    b, h, s, dk = q.shape
    dv = v.shape[-1]
    nc = s // GLA_CHUNK
    f32 = jnp.float32
    q = q.astype(f32).reshape(b, h, nc, GLA_CHUNK, dk) * (dk ** -0.5)
    k = k.astype(f32).reshape(b, h, nc, GLA_CHUNK, dk)
    v = v.astype(f32).reshape(b, h, nc, GLA_CHUNK, dv)
    G = jnp.cumsum(log_g.astype(f32).reshape(b, h, nc, GLA_CHUNK, dk), axis=3)
    G_last = G[:, :, :, -1]
    q_dec = q * jnp.exp(G)
    k_dec = k * jnp.exp(-G)
    causal = jnp.tril(jnp.ones((GLA_CHUNK, GLA_CHUNK), dtype=bool))
    scores = jnp.where(causal, jnp.einsum('bhnid,bhnjd->bhnij', q_dec, k_dec), 0.0)
    o_intra = jnp.einsum('bhnij,bhnje->bhnie', scores, v)
    k_to_end = k * jnp.exp(G_last[:, :, :, None] - G)
    chunk_kv = jnp.einsum('bhnjd,bhnje->bhnde', k_to_end, v)
    decay = jnp.exp(G_last)

    def step(state, inp):
        dec, kv = inp
        return state * dec[..., None] + kv, state

    init = jnp.zeros((b, h, dk, dv), f32)
    _, prev_states = lax.scan(step, init, (jnp.moveaxis(decay, 2, 0), jnp.moveaxis(chunk_kv, 2, 0)))
    prev_states = jnp.moveaxis(prev_states, 0, 2)
    o_inter = jnp.einsum('bhnid,bhnde->bhnie', q_dec, prev_states)
    return (o_intra + o_inter).reshape(b, h, s, dv)


def hybrid_mixer(h, w_in, pool_w, pool_scale, gk_w2, gk_b, gla_norm_w, w_out):
    b, s, _ = h.shape
    proj = h @ w_in
    p = proj[..., :OFF_Q]
    q = proj[..., OFF_Q:OFF_K]
    k = proj[..., OFF_K:OFF_V]
    v = proj[..., OFF_V:OFF_G]
    g_low = proj[..., OFF_G:OFF_R]
    r = proj[..., OFF_R:]
    pool_out = multiscale_pool(p, pool_w, pool_scale)
    log_g = jax.nn.log_sigmoid((g_low @ gk_w2 + gk_b).astype(jnp.float32)) / GLA_GATE_NORMALIZER

    def heads(t, d):
        return t.reshape(b, s, GLA_HEADS, d).transpose(0, 2, 1, 3)

    o = gla_chunked(heads(q, GLA_DK), heads(k, GLA_DK), heads(v, GLA_DV), heads(log_g, GLA_DK))
    o = rms_norm(o, gla_norm_w).transpose(0, 2, 1, 3).reshape(b, s, GLA_VALUE_WIDTH).astype(h.dtype)
    gla_out = o * jax.nn.silu(r)
    return jnp.concatenate([pool_out, gla_out], axis=-1) @ w_out


def memory_cross_attention(h, mem_n, wq, wkv, wo):
    b, s, _ = h.shape
    m = mem_n.shape[1]
    q = (h @ wq).reshape(b, s, XATTN_HEADS, XATTN_HEAD_DIM)
    kv = mem_n @ wkv
    k = kv[..., :D_MODEL].reshape(b, m, XATTN_HEADS, XATTN_HEAD_DIM)
    v = kv[..., D_MODEL:].reshape(b, m, XATTN_HEADS, XATTN_HEAD_DIM)
    scores = jnp.einsum('bshd,bmhd->bhsm', q, k).astype(jnp.float32) * (XATTN_HEAD_DIM ** -0.5)
    probs = jax.nn.softmax(scores, axis=-1).astype(v.dtype)
    o = jnp.einsum('bhsm,bmhd->bshd', probs, v).reshape(b, s, D_MODEL)
    return o @ wo


def conv_ffn(h, w_up, conv_w, conv_b, w_down):
    s = h.shape[1]
    u = h @ w_up
    u_pad = jnp.pad(u, ((0, 0), (CONV_WIDTH - 1, 0), (0, 0)))
    u = conv_b + conv_w[0] * u_pad[:, 0:s] + conv_w[1] * u_pad[:, 1:1 + s] + conv_w[2] * u_pad[:, 2:2 + s]
    gate = u[..., :D_FF]
    val = u[..., D_FF:]
    return (jax.nn.silu(gate) * val) @ w_down


def setup_inputs(seed: int = 0) -> dict:
    key = jax.random.key(seed)
    ks = jax.random.split(key, 24)
    f32 = jnp.float32
    L = DEPTH

    def nrm(k, shape, scale):
        return jax.random.normal(k, shape, f32) * scale

    def gain(k, shape):
        return 1.0 + 0.05 * jax.random.normal(k, shape, f32)

    return {
        'x': jax.random.normal(ks[0], (BATCH, SEQ, D_MODEL), f32),
        'mem': jax.random.normal(ks[1], (BATCH, N_MEM, D_MODEL), f32),
        'norm_mix_w': gain(ks[2], (L, D_MODEL)),
        'w_in': nrm(ks[3], (L, D_MODEL, IN_PROJ_WIDTH), D_MODEL ** -0.5),
        'pool_w': nrm(ks[4], (L, POOL_GROUPS, POOL_GROUP_DIM, POOL_GROUP_DIM), POOL_GROUP_DIM ** -0.5),
        'pool_scale': 0.5 + 0.05 * jax.random.normal(ks[5], (L, POOL_WIDTH), f32),
        'gk_w2': nrm(ks[6], (L, GLA_GATE_RANK, GLA_KEY_WIDTH), GLA_GATE_RANK ** -0.5),
        'gk_b': nrm(ks[7], (L, GLA_KEY_WIDTH), 0.1),
        'gla_norm_w': gain(ks[8], (L, GLA_DV)),
        'w_out': nrm(ks[9], (L, MIX_WIDTH, D_MODEL), MIX_WIDTH ** -0.5),
        'norm_xattn_w': gain(ks[10], (L, D_MODEL)),
        'norm_mem_w': gain(ks[11], (L, D_MODEL)),
        'xattn_wq': nrm(ks[12], (L, D_MODEL, D_MODEL), D_MODEL ** -0.5),
        'xattn_wkv': nrm(ks[13], (L, D_MODEL, 2 * D_MODEL), D_MODEL ** -0.5),
        'xattn_wo': nrm(ks[14], (L, D_MODEL, D_MODEL), D_MODEL ** -0.5),
        'norm_ffn_w': gain(ks[15], (L, D_MODEL)),
        'ffn_w_up': nrm(ks[16], (L, D_MODEL, 2 * D_FF), D_MODEL ** -0.5),
        'ffn_conv_w': nrm(ks[17], (L, CONV_WIDTH, 2 * D_FF), CONV_WIDTH ** -0.5),
        'ffn_conv_b': nrm(ks[18], (L, 2 * D_FF), 0.01),
        'ffn_w_down': nrm(ks[19], (L, D_FF, D_MODEL), D_FF ** -0.5),
        'norm_final_w': gain(ks[20], (D_MODEL,)),
    }


def reference(x, mem, norm_mix_w, w_in, pool_w, pool_scale, gk_w2, gk_b, gla_norm_w, w_out,
              norm_xattn_w, norm_mem_w, xattn_wq, xattn_wkv, xattn_wo,
              norm_ffn_w, ffn_w_up, ffn_conv_w, ffn_conv_b, ffn_w_down, norm_final_w):
    for l in range(DEPTH):
        h = rms_norm(x, norm_mix_w[l])
        x = x + hybrid_mixer(h, w_in[l], pool_w[l], pool_scale[l], gk_w2[l], gk_b[l], gla_norm_w[l], w_out[l])
        h = rms_norm(x, norm_xattn_w[l])
        m = rms_norm(mem, norm_mem_w[l])
        x = x + memory_cross_attention(h, m, xattn_wq[l], xattn_wkv[l], xattn_wo[l])
        h = rms_norm(x, norm_ffn_w[l])
        x = x + conv_ffn(h, ffn_w_up[l], ffn_conv_w[l], ffn_conv_b[l], ffn_w_down[l])
    return rms_norm(x, norm_final_w)
```

```python
import functools

import jax
import jax.numpy as jnp
from jax import lax
from jax.experimental import pallas as pl
from jax.experimental.pallas import tpu as pltpu

D_MODEL = 1024
N_MEM = 256
POOL_WIDTH = 512
POOL_GROUPS = 4
POOL_GROUP_DIM = 128
POOL_WINDOWS = (2, 4, 8, 16)
POOL_HALO = 16
GLA_VALUE_WIDTH = 512
GLA_KEY_WIDTH = 256
GLA_HEADS = 4
GLA_DK = 64
GLA_DV = 128
GLA_GATE_RANK = 16
GLA_GATE_NORMALIZER = 16.0
GLA_CHUNK = 64
XATTN_HEADS = 4
XATTN_HEAD_DIM = 256
D_FF = 2816
CONV_WIDTH = 3
EPS = 1e-6

LANES = 128
SUBLANES = 8
FF_CHUNK = 256
VMEM_LIMIT_BYTES = 56 * 1024 * 1024

BF16 = jnp.bfloat16
F32 = jnp.float32

_NT = (((1,), (1,)), ((), ()))
_TN = (((0,), (0,)), ((), ()))


def _dot(a, b):
    return jnp.dot(a, b, preferred_element_type=F32)


def _rms(x, w):
    return x * lax.rsqrt(jnp.mean(x * x, axis=-1, keepdims=True) + EPS) * w


def _sigmoid(x):
    return 1.0 / (1.0 + jnp.exp(-x))


def _mixer_kernel(x_ref, nw_ref, wp_ref, wqk_ref, wv_ref, wg_ref, wr_ref, poolw_ref, pscale_ref,
                  gkw2_ref, gkb_ref, gnw_ref, wout_ref, o_ref,
                  pcarry, state, q_s, k_s, v_s, g_s, o_s, *, ts):
    s = pl.program_id(1)

    @pl.when(s == 0)
    def _():
        pcarry[...] = jnp.zeros_like(pcarry)
        state[...] = jnp.zeros_like(state)

    x = x_ref[0]
    h = _rms(x, nw_ref[...]).astype(BF16)

    p = _dot(h, wp_ref[...])
    pe = jnp.concatenate([pcarry[...], p], axis=0)
    pcarry[...] = p[ts - POOL_HALO:, :]
    e2 = pe + pltpu.roll(pe, 1, 0)
    e4 = e2 + pltpu.roll(e2, 2, 0)
    e8 = e4 + pltpu.roll(e4, 4, 0)
    e16 = e8 + pltpu.roll(e8, 8, 0)
    t = s * ts + lax.broadcasted_iota(jnp.int32, (ts, 1), 0)
    pool_parts = []
    for g, (w, e) in enumerate(zip(POOL_WINDOWS, (e2, e4, e8, e16))):
        lo, hi = g * POOL_GROUP_DIM, (g + 1) * POOL_GROUP_DIM
        cnt = jnp.minimum(t + 1, w).astype(F32)
        pooled = e[POOL_HALO:, lo:hi] / cnt - p[:, lo:hi]
        mixed = _dot(pooled.astype(BF16), poolw_ref[g])
        pool_parts.append(mixed * pscale_ref[:, lo:hi])

    qk = _dot(h, wqk_ref[...])
    q_s[...] = qk[:, :GLA_KEY_WIDTH]
    k_s[...] = qk[:, GLA_KEY_WIDTH:]
    v_s[...] = _dot(h, wv_ref[...])
    g_low = _dot(h, wg_ref[...])
    gate = _dot(g_low.astype(BF16), gkw2_ref[...]) + gkb_ref[...]
    lg = (jnp.minimum(gate, 0.0) - jnp.log1p(jnp.exp(-jnp.abs(gate)))) * (1.0 / GLA_GATE_NORMALIZER)
    crow = lax.broadcasted_iota(jnp.int32, (ts, 1), 0) % GLA_CHUNK
    G = lg
    shift = 1
    while shift < GLA_CHUNK:
        G = G + jnp.where(crow >= shift, pltpu.roll(G, shift, 0), 0.0)
        shift *= 2
    g_s[...] = G

    lane = lax.broadcasted_iota(jnp.int32, (1, LANES), 1)
    first_head = lane < GLA_DK
    ci = lax.broadcasted_iota(jnp.int32, (GLA_CHUNK, GLA_CHUNK), 0)
    cj = lax.broadcasted_iota(jnp.int32, (GLA_CHUNK, GLA_CHUNK), 1)
    causal = ci >= cj

    def chunk_body(c, carry):
        r0 = pl.multiple_of(c * GLA_CHUNK, GLA_CHUNK)
        rows = pl.ds(r0, GLA_CHUNK)
        Gc = g_s[rows, :]
        qc = q_s[rows, :]
        kc = k_s[rows, :]
        g_last = Gc[GLA_CHUNK - 1:GLA_CHUNK, :]
        q_dec = qc * (GLA_DK ** -0.5) * jnp.exp(Gc)
        k_dec = (kc * jnp.exp(-Gc)).astype(BF16)
        k_end = (kc * jnp.exp(g_last - Gc)).astype(BF16)
        decay = jnp.exp(g_last)
        for j in range(GLA_HEADS // 2):
            kl = slice(j * LANES, (j + 1) * LANES)
            st = state[j]
            st_b = st.astype(BF16)
            new_kv = []
            for hh in range(2):
                hd = 2 * j + hh
                vl = slice(hd * GLA_DV, (hd + 1) * GLA_DV)
                keep = first_head if hh == 0 else jnp.logical_not(first_head)
                qm = jnp.where(keep, q_dec[:, kl], 0.0).astype(BF16)
                sc = lax.dot_general(qm, k_dec[:, kl], _NT, preferred_element_type=F32)
                sc = jnp.where(causal, sc, 0.0).astype(BF16)
                vh = v_s[rows, vl].astype(BF16)
                o_intra = _dot(sc, vh)
                o_inter = lax.dot_general(qm, st_b, _NT, preferred_element_type=F32)
                o_s[rows, vl] = o_intra + o_inter
                new_kv.append(lax.dot_general(vh, k_end[:, kl], _TN, preferred_element_type=F32))
            state[j] = st * decay[:, kl] + jnp.where(first_head, new_kv[0], new_kv[1])
        return carry

    lax.fori_loop(0, ts // GLA_CHUNK, chunk_body, 0)

    r = _dot(h, wr_ref[...])
    gla_parts = []
    for hd in range(GLA_HEADS):
        vl = slice(hd * GLA_DV, (hd + 1) * GLA_DV)
        oh = _rms(o_s[:, vl], gnw_ref[...])
        rh = r[:, vl]
        gla_parts.append(oh * (rh * _sigmoid(rh)))
    cat = jnp.concatenate(pool_parts + gla_parts, axis=1).astype(BF16)
    o_ref[0] = x + _dot(cat, wout_ref[...])


def _mixer(x, nw, wp, wqk, wv, wg, wr, poolw, pscale, gkw2, gkb, gnw, wout, *, ts):
    b, s, d = x.shape
    const = lambda shape: pl.BlockSpec(shape, lambda i, j: (0,) * len(shape))
    return pl.pallas_call(
        functools.partial(_mixer_kernel, ts=ts),
        out_shape=jax.ShapeDtypeStruct(x.shape, F32),
        grid=(b, s // ts),
        in_specs=[
            pl.BlockSpec((1, ts, d), lambda i, j: (i, j, 0)),
            const(nw.shape), const(wp.shape), const(wqk.shape), const(wv.shape), const(wg.shape),
            const(wr.shape), const(poolw.shape), const(pscale.shape), const(gkw2.shape), const(gkb.shape),
            const(gnw.shape), const(wout.shape),
        ],
        out_specs=pl.BlockSpec((1, ts, d), lambda i, j: (i, j, 0)),
        scratch_shapes=[
            pltpu.VMEM((POOL_HALO, POOL_WIDTH), F32),
            pltpu.VMEM((GLA_HEADS // 2, GLA_DV, LANES), F32),
            pltpu.VMEM((ts, GLA_KEY_WIDTH), F32),
            pltpu.VMEM((ts, GLA_KEY_WIDTH), F32),
            pltpu.VMEM((ts, GLA_VALUE_WIDTH), F32),
            pltpu.VMEM((ts, GLA_KEY_WIDTH), F32),
            pltpu.VMEM((ts, GLA_VALUE_WIDTH), F32),
        ],
        compiler_params=pltpu.CompilerParams(
            dimension_semantics=("parallel", "arbitrary"), vmem_limit_bytes=VMEM_LIMIT_BYTES),
        name="mixer",
    )(x, nw, wp, wqk, wv, wg, wr, poolw, pscale, gkw2, gkb, gnw, wout)


def _mem_kv_kernel(mem_ref, nw_ref, wk_ref, wv_ref, kt_ref, v_ref):
    m = _rms(mem_ref[0], nw_ref[...]).astype(BF16)
    k = _dot(m, wk_ref[...])
    kt_ref[0] = k.T.astype(BF16)
    v_ref[0] = _dot(m, wv_ref[...]).astype(BF16)


def _mem_kv(mem, nw, wk, wv):
    b, m, d = mem.shape
    const = lambda shape: pl.BlockSpec(shape, lambda i: (0,) * len(shape))
    return pl.pallas_call(
        _mem_kv_kernel,
        out_shape=(jax.ShapeDtypeStruct((b, d, m), BF16), jax.ShapeDtypeStruct((b, m, d), BF16)),
        grid=(b,),
        in_specs=[pl.BlockSpec((1, m, d), lambda i: (i, 0, 0)), const(nw.shape), const(wk.shape), const(wv.shape)],
        out_specs=(pl.BlockSpec((1, d, m), lambda i: (i, 0, 0)), pl.BlockSpec((1, m, d), lambda i: (i, 0, 0))),
        compiler_params=pltpu.CompilerParams(
            dimension_semantics=("parallel",), vmem_limit_bytes=VMEM_LIMIT_BYTES),
        name="mem_kv",
    )(mem, nw, wk, wv)


def _xattn_kernel(x_ref, nw_ref, wq_ref, kt_ref, v_ref, wo_ref, o_ref):
    x = x_ref[0]
    h = _rms(x, nw_ref[...]).astype(BF16)
    q = (_dot(h, wq_ref[...]) * (XATTN_HEAD_DIM ** -0.5)).astype(BF16)
    outs = []
    for hd in range(XATTN_HEADS):
        sl = slice(hd * XATTN_HEAD_DIM, (hd + 1) * XATTN_HEAD_DIM)
        sc = _dot(q[:, sl], kt_ref[0, sl, :])
        e = jnp.exp(sc - jnp.max(sc, axis=-1, keepdims=True))
        probs = e / jnp.sum(e, axis=-1, keepdims=True)
        outs.append(_dot(probs.astype(BF16), v_ref[0, :, sl]))
    o = jnp.concatenate(outs, axis=1).astype(BF16)
    o_ref[0] = x + _dot(o, wo_ref[...])


def _xattn(x, nw, wq, kt, v, wo, *, ts):
    b, s, d = x.shape
    const = lambda shape: pl.BlockSpec(shape, lambda i, j: (0,) * len(shape))
    return pl.pallas_call(
        _xattn_kernel,
        out_shape=jax.ShapeDtypeStruct(x.shape, F32),
        grid=(b, s // ts),
        in_specs=[
            pl.BlockSpec((1, ts, d), lambda i, j: (i, j, 0)),
            const(nw.shape), const(wq.shape),
            pl.BlockSpec((1, d, N_MEM), lambda i, j: (i, 0, 0)),
            pl.BlockSpec((1, N_MEM, d), lambda i, j: (i, 0, 0)),
            const(wo.shape),
        ],
        out_specs=pl.BlockSpec((1, ts, d), lambda i, j: (i, j, 0)),
        compiler_params=pltpu.CompilerParams(
            dimension_semantics=("parallel", "parallel"), vmem_limit_bytes=VMEM_LIMIT_BYTES),
        name="xattn",
    )(x, nw, wq, kt, v, wo)


def _ffn_kernel(x_ref, nw_ref, wup_ref, cw_ref, cb_ref, wdn_ref, nfw_ref, o_ref, ucarry, *, ts):
    s = pl.program_id(1)

    @pl.when(s == 0)
    def _():
        ucarry[...] = jnp.zeros_like(ucarry)

    x = x_ref[0]
    h = _rms(x, nw_ref[...]).astype(BF16)
    row = lax.broadcasted_iota(jnp.int32, (ts, 1), 0)

    def conv_cols(col0):
        cols = slice(col0, col0 + FF_CHUNK)
        u = _dot(h, wup_ref[:, cols])
        prev = ucarry[:, cols]
        ucarry[:, cols] = u[ts - SUBLANES:, :]
        pm1 = prev[SUBLANES - 1:SUBLANES, :]
        pm2 = prev[SUBLANES - 2:SUBLANES - 1, :]
        u1 = jnp.where(row == 0, pm1, pltpu.roll(u, 1, 0))
        u2 = jnp.where(row == 0, pm2, jnp.where(row == 1, pm1, pltpu.roll(u, 2, 0)))
        cw = cw_ref[:, cols]
        return cb_ref[:, cols] + cw[0:1, :] * u2 + cw[1:2, :] * u1 + cw[2:3, :] * u

    acc = jnp.zeros((ts, D_MODEL), F32)
    for j in range(D_FF // FF_CHUNK):
        gate = conv_cols(j * FF_CHUNK)
        val = conv_cols(D_FF + j * FF_CHUNK)
        act = (gate * _sigmoid(gate) * val).astype(BF16)
        acc = acc + _dot(act, wdn_ref[j * FF_CHUNK:(j + 1) * FF_CHUNK, :])
    o_ref[0] = _rms(x + acc, nfw_ref[...])


def _ffn(x, nw, wup, cw, cb, wdn, nfw, *, ts):
    b, s, d = x.shape
    const = lambda shape: pl.BlockSpec(shape, lambda i, j: (0,) * len(shape))
    return pl.pallas_call(
        functools.partial(_ffn_kernel, ts=ts),
        out_shape=jax.ShapeDtypeStruct(x.shape, F32),
        grid=(b, s // ts),
        in_specs=[
            pl.BlockSpec((1, ts, d), lambda i, j: (i, j, 0)),
            const(nw.shape), const(wup.shape), const(cw.shape), const(cb.shape), const(wdn.shape),
            const(nfw.shape),
        ],
        out_specs=pl.BlockSpec((1, ts, d), lambda i, j: (i, j, 0)),
        scratch_shapes=[pltpu.VMEM((SUBLANES, 2 * D_FF), F32)],
        compiler_params=pltpu.CompilerParams(
            dimension_semantics=("parallel", "arbitrary"), vmem_limit_bytes=VMEM_LIMIT_BYTES),
        name="ffn",
    )(x, nw, wup, cw, cb, wdn, nfw)


def kernel(x, mem, norm_mix_w, w_in, pool_w, pool_scale, gk_w2, gk_b, gla_norm_w, w_out, norm_xattn_w,
           norm_mem_w, xattn_wq, xattn_wkv, xattn_wo, norm_ffn_w, ffn_w_up, ffn_conv_w, ffn_conv_b,
           ffn_w_down, norm_final_w):
    depth = w_in.shape[0]
    off_q = POOL_WIDTH
    off_v = off_q + 2 * GLA_KEY_WIDTH
    off_g = off_v + GLA_VALUE_WIDTH
    off_r = off_g + GLA_GATE_RANK
    row2 = lambda a: a.reshape(1, -1)
    for l in range(depth):
        wi = w_in[l]
        wg = jnp.pad(wi[:, off_g:off_r], ((0, 0), (0, LANES - GLA_GATE_RANK))).astype(BF16)
        gkw2 = jnp.pad(gk_w2[l], ((0, LANES - GLA_GATE_RANK), (0, 0))).astype(BF16)
        x = _mixer(
            x, row2(norm_mix_w[l]), wi[:, :off_q].astype(BF16), wi[:, off_q:off_v].astype(BF16),
            wi[:, off_v:off_g].astype(BF16), wg, wi[:, off_r:].astype(BF16), pool_w[l].astype(BF16),
            row2(pool_scale[l]), gkw2, row2(gk_b[l]), row2(gla_norm_w[l]), w_out[l].astype(BF16), ts=512)
        wkv = xattn_wkv[l]
        kt, v = _mem_kv(mem, row2(norm_mem_w[l]), wkv[:, :D_MODEL].astype(BF16), wkv[:, D_MODEL:].astype(BF16))
        x = _xattn(x, row2(norm_xattn_w[l]), xattn_wq[l].astype(BF16), kt, v, xattn_wo[l].astype(BF16), ts=512)
        assert depth == 1
        x = _ffn(x, row2(norm_ffn_w[l]), ffn_w_up[l].astype(BF16), ffn_conv_w[l], row2(ffn_conv_b[l]),
                 ffn_w_down[l].astype(BF16), row2(norm_final_w), ts=512)
    return x
```

```python
import functools

import jax
import jax.numpy as jnp
from jax import lax
from jax.experimental import pallas as pl
from jax.experimental.pallas import tpu as pltpu

D_MODEL = 1024
N_MEM = 256
POOL_WIDTH = 512
POOL_GROUPS = 4
POOL_GROUP_DIM = 128
POOL_WINDOWS = (2, 4, 8, 16)
POOL_HALO = 16
GLA_VALUE_WIDTH = 512
GLA_KEY_WIDTH = 256
GLA_HEADS = 4
GLA_DK = 64
GLA_DV = 128
GLA_GATE_RANK = 16
CUM_PAD = 32
GLA_GATE_NORMALIZER = 16.0
GLA_CHUNK = 64
XATTN_HEADS = 4
XATTN_HEAD_DIM = 256
D_FF = 2816
CONV_WIDTH = 3
EPS = 1e-6

LANES = 128
SUBLANES = 8
FF_CHUNK = 256
FF_UBUFS = 4
VMEM_LIMIT_BYTES = 56 * 1024 * 1024

BF16 = jnp.bfloat16
F32 = jnp.float32

_NT = (((1,), (1,)), ((), ()))
_TN = (((0,), (0,)), ((), ()))


def _dot(a, b):
    return jnp.dot(a, b, preferred_element_type=F32)


def _rms(x, w):
    return x * lax.rsqrt(jnp.mean(x * x, axis=-1, keepdims=True) + EPS) * w


def _sigmoid(x):
    return 1.0 / (1.0 + jnp.exp(-x))


def _mixer_kernel(x_ref, nw_ref, wp_ref, wqk_ref, wv_ref, wg_ref, wr_ref, poolw_ref, pscale_ref,
                  gkw2_ref, gkb_ref, gnw_ref, wout_ref, o_ref,
                  pbuf, gbuf, state, q_s, k_s, v_s, rs_s, cat_s, *, ts):
    s = pl.program_id(1)

    @pl.when(s == 0)
    def _():
        pbuf[:, :POOL_HALO, :] = jnp.zeros((POOL_GROUPS, POOL_HALO, LANES), F32)
        state[...] = jnp.zeros_like(state)

    x = x_ref[0]
    h = _rms(x, nw_ref[...]).astype(BF16)
    row = lax.broadcasted_iota(jnp.int32, (ts, LANES), 0)

    g_low = _dot(h, wg_ref[...])
    gate = _dot(g_low.astype(BF16), gkw2_ref[...]) + gkb_ref[...]
    lg = (jnp.minimum(gate, 0.0) - jnp.log1p(jnp.exp(-jnp.abs(gate)))) * (1.0 / GLA_GATE_NORMALIZER)
    crow = row % GLA_CHUNK
    for k in range(GLA_HEADS // 2):
        gb = gbuf.at[k]
        gb[:CUM_PAD, :] = jnp.zeros((CUM_PAD, LANES), F32)
        gb[CUM_PAD:, :] = lg[:, k * LANES:(k + 1) * LANES]
    shift = 1
    while shift < GLA_CHUNK:
        for k in range(GLA_HEADS // 2):
            gb = gbuf.at[k]
            cur = gb[CUM_PAD:, :]
            prev = gb[CUM_PAD - shift:CUM_PAD - shift + ts, :]
            gb[CUM_PAD:, :] = cur + jnp.where(crow >= shift, prev, 0.0)
        shift *= 2

    p = _dot(h, wp_ref[...])
    t1 = s * ts + row + 1
    for g, w in enumerate(POOL_WINDOWS):
        lo, hi = g * POOL_GROUP_DIM, (g + 1) * POOL_GROUP_DIM
        pg = p[:, lo:hi]
        buf = pbuf.at[g]
        buf[POOL_HALO:, :] = pg
        win = pg
        for d in range(1, w):
            win = win + buf[POOL_HALO - d:POOL_HALO - d + ts, :]
        buf[:POOL_HALO, :] = pg[ts - POOL_HALO:, :]
        cnt = jnp.minimum(t1, w).astype(F32)
        pooled = win / cnt - pg
        mixed = _dot(pooled.astype(BF16), poolw_ref[g])
        cat_s[:, lo:hi] = (mixed * pscale_ref[:, lo:hi]).astype(BF16)

    qk = _dot(h, wqk_ref[...])
    q_s[...] = qk[:, :GLA_KEY_WIDTH]
    k_s[...] = qk[:, GLA_KEY_WIDTH:]
    v_s[...] = _dot(h, wv_ref[...]).astype(BF16)
    r = _dot(h, wr_ref[...])
    rs_s[...] = r * _sigmoid(r)
    lane = lax.broadcasted_iota(jnp.int32, (1, LANES), 1)
    first_head = lane < GLA_DK
    ci = lax.broadcasted_iota(jnp.int32, (2 * GLA_CHUNK, GLA_CHUNK), 0) % GLA_CHUNK
    cj = lax.broadcasted_iota(jnp.int32, (2 * GLA_CHUNK, GLA_CHUNK), 1)
    causal2 = ci >= cj

    for j in range(GLA_HEADS // 2):
        kl = slice(j * LANES, (j + 1) * LANES)
        st = state[j]
        for c in range(ts // GLA_CHUNK):
            rows = slice(c * GLA_CHUNK, (c + 1) * GLA_CHUNK)
            Gc = gbuf[j, CUM_PAD + c * GLA_CHUNK:CUM_PAD + (c + 1) * GLA_CHUNK, :]
            qc = q_s[rows, kl]
            kc = k_s[rows, kl]
            g_last = Gc[GLA_CHUNK - 1:GLA_CHUNK, :]
            q_dec = qc * (GLA_DK ** -0.5) * jnp.exp(Gc)
            k_dec = (kc * jnp.exp(-Gc)).astype(BF16)
            k_end = (kc * jnp.exp(g_last - Gc)).astype(BF16)
            decay = jnp.exp(g_last)
            qstack = jnp.concatenate([jnp.where(first_head, q_dec, 0.0), jnp.where(first_head, 0.0, q_dec)],
                                     axis=0).astype(BF16)
            sc = lax.dot_general(qstack, k_dec, _NT, preferred_element_type=F32)
            sc = jnp.where(causal2, sc, 0.0).astype(BF16)
            o_inter = lax.dot_general(qstack, st.astype(BF16), _NT, preferred_element_type=F32)
            vpair = v_s[rows, 2 * j * GLA_DV:2 * (j + 1) * GLA_DV]
            for hh in range(2):
                hd = 2 * j + hh
                hr = slice(hh * GLA_CHUNK, (hh + 1) * GLA_CHUNK)
                o = _dot(sc[hr, :], vpair[:, hh * GLA_DV:(hh + 1) * GLA_DV]) + o_inter[hr, :]
                gl = _rms(o, gnw_ref[...]) * rs_s[rows, hd * GLA_DV:(hd + 1) * GLA_DV]
                cat_s[rows, POOL_WIDTH + hd * GLA_DV:POOL_WIDTH + (hd + 1) * GLA_DV] = gl.astype(BF16)
            kv = lax.dot_general(vpair, k_end, _TN, preferred_element_type=F32)
            st = st * decay + jnp.where(first_head, kv[:GLA_DV, :], kv[GLA_DV:, :])
        state[j] = st

    o_ref[0] = x + _dot(cat_s[...], wout_ref[...])


def _mixer(x, nw, wp, wqk, wv, wg, wr, poolw, pscale, gkw2, gkb, gnw, wout, *, ts):
    b, s, d = x.shape
    const = lambda shape: pl.BlockSpec(shape, lambda i, j: (0,) * len(shape))
    return pl.pallas_call(
        functools.partial(_mixer_kernel, ts=ts),
        out_shape=jax.ShapeDtypeStruct(x.shape, F32),
        grid=(b, s // ts),
        in_specs=[
            pl.BlockSpec((1, ts, d), lambda i, j: (i, j, 0)),
            const(nw.shape), const(wp.shape), const(wqk.shape), const(wv.shape), const(wg.shape),
            const(wr.shape), const(poolw.shape), const(pscale.shape), const(gkw2.shape), const(gkb.shape),
            const(gnw.shape), const(wout.shape),
        ],
        out_specs=pl.BlockSpec((1, ts, d), lambda i, j: (i, j, 0)),
        scratch_shapes=[
            pltpu.VMEM((POOL_GROUPS, POOL_HALO + ts, LANES), F32),
            pltpu.VMEM((GLA_HEADS // 2, CUM_PAD + ts, LANES), F32),
            pltpu.VMEM((GLA_HEADS // 2, GLA_DV, LANES), F32),
            pltpu.VMEM((ts, GLA_KEY_WIDTH), F32),
            pltpu.VMEM((ts, GLA_KEY_WIDTH), F32),
            pltpu.VMEM((ts, GLA_VALUE_WIDTH), BF16),
            pltpu.VMEM((ts, GLA_VALUE_WIDTH), F32),
            pltpu.VMEM((ts, D_MODEL), BF16),
        ],
        compiler_params=pltpu.CompilerParams(
            dimension_semantics=("parallel", "arbitrary"), vmem_limit_bytes=VMEM_LIMIT_BYTES),
        name="mixer",
    )(x, nw, wp, wqk, wv, wg, wr, poolw, pscale, gkw2, gkb, gnw, wout)


def _mem_kv_kernel(mem_ref, nw_ref, wk_ref, wv_ref, kt_ref, v_ref):
    m = _rms(mem_ref[0], nw_ref[...]).astype(BF16)
    k = _dot(m, wk_ref[...])
    kt_ref[0] = k.T.astype(BF16)
    v_ref[0] = _dot(m, wv_ref[...]).astype(BF16)


def _mem_kv(mem, nw, wk, wv):
    b, m, d = mem.shape
    const = lambda shape: pl.BlockSpec(shape, lambda i: (0,) * len(shape))
    return pl.pallas_call(
        _mem_kv_kernel,
        out_shape=(jax.ShapeDtypeStruct((b, d, m), BF16), jax.ShapeDtypeStruct((b, m, d), BF16)),
        grid=(b,),
        in_specs=[pl.BlockSpec((1, m, d), lambda i: (i, 0, 0)), const(nw.shape), const(wk.shape), const(wv.shape)],
        out_specs=(pl.BlockSpec((1, d, m), lambda i: (i, 0, 0)), pl.BlockSpec((1, m, d), lambda i: (i, 0, 0))),
        compiler_params=pltpu.CompilerParams(
            dimension_semantics=("parallel",), vmem_limit_bytes=VMEM_LIMIT_BYTES),
        name="mem_kv",
    )(mem, nw, wk, wv)


def _xattn_kernel(x_ref, nw_ref, wq_ref, kt_ref, v_ref, wo_ref, o_ref):
    x = x_ref[0]
    h = _rms(x, nw_ref[...]).astype(BF16)
    q = (_dot(h, wq_ref[...]) * (XATTN_HEAD_DIM ** -0.5)).astype(BF16)
    outs = []
    for hd in range(XATTN_HEADS):
        sl = slice(hd * XATTN_HEAD_DIM, (hd + 1) * XATTN_HEAD_DIM)
        sc = _dot(q[:, sl], kt_ref[0, sl, :])
        e = jnp.exp(sc - jnp.max(sc, axis=-1, keepdims=True))
        probs = e * (1.0 / jnp.sum(e, axis=-1, keepdims=True))
        outs.append(_dot(probs.astype(BF16), v_ref[0, :, sl]))
    o = jnp.concatenate(outs, axis=1).astype(BF16)
    o_ref[0] = x + _dot(o, wo_ref[...])


def _xattn(x, nw, wq, kt, v, wo, *, ts):
    b, s, d = x.shape
    const = lambda shape: pl.BlockSpec(shape, lambda i, j: (0,) * len(shape))
    return pl.pallas_call(
        _xattn_kernel,
        out_shape=jax.ShapeDtypeStruct(x.shape, F32),
        grid=(b, s // ts),
        in_specs=[
            pl.BlockSpec((1, ts, d), lambda i, j: (i, j, 0)),
            const(nw.shape), const(wq.shape),
            pl.BlockSpec((1, d, N_MEM), lambda i, j: (i, 0, 0)),
            pl.BlockSpec((1, N_MEM, d), lambda i, j: (i, 0, 0)),
            const(wo.shape),
        ],
        out_specs=pl.BlockSpec((1, ts, d), lambda i, j: (i, j, 0)),
        compiler_params=pltpu.CompilerParams(
            dimension_semantics=("parallel", "parallel"), vmem_limit_bytes=VMEM_LIMIT_BYTES),
        name="xattn",
    )(x, nw, wq, kt, v, wo)


def _ffn_kernel(x_ref, nw_ref, wup_ref, cw_ref, cb_ref, wdn_ref, nfw_ref, o_ref, ucarry, ubuf, act_s, *, ts):
    s = pl.program_id(1)

    @pl.when(s == 0)
    def _():
        ucarry[...] = jnp.zeros_like(ucarry)

    x = x_ref[0]
    h = _rms(x, nw_ref[...]).astype(BF16)

    def conv_cols(col0, slot):
        u = _dot(h, wup_ref[:, col0:col0 + FF_CHUNK])
        outs = []
        for k in range(FF_CHUNK // LANES):
            cols = slice(col0 + k * LANES, col0 + (k + 1) * LANES)
            uk = u[:, k * LANES:(k + 1) * LANES]
            buf = ubuf.at[slot, k]
            buf[:SUBLANES, :] = ucarry[:, cols]
            buf[SUBLANES:, :] = uk
            ucarry[:, cols] = uk[ts - SUBLANES:, :]
            cw = cw_ref[:, cols]
            outs.append(cb_ref[:, cols] + cw[0:1, :] * buf[SUBLANES - 2:SUBLANES - 2 + ts, :]
                        + cw[1:2, :] * buf[SUBLANES - 1:SUBLANES - 1 + ts, :] + cw[2:3, :] * uk)
        return jnp.concatenate(outs, axis=1)

    for j in range(D_FF // FF_CHUNK):
        gate = conv_cols(j * FF_CHUNK, (2 * j) % FF_UBUFS)
        val = conv_cols(D_FF + j * FF_CHUNK, (2 * j + 1) % FF_UBUFS)
        act_s[:, j * FF_CHUNK:(j + 1) * FF_CHUNK] = (gate * _sigmoid(gate) * val).astype(BF16)
    o_ref[0] = _rms(x + _dot(act_s[...], wdn_ref[...]), nfw_ref[...])


def _ffn(x, nw, wup, cw, cb, wdn, nfw, *, ts):
    b, s, d = x.shape
    const = lambda shape: pl.BlockSpec(shape, lambda i, j: (0,) * len(shape))
    return pl.pallas_call(
        functools.partial(_ffn_kernel, ts=ts),
        out_shape=jax.ShapeDtypeStruct(x.shape, F32),
        grid=(b, s // ts),
        in_specs=[
            pl.BlockSpec((1, ts, d), lambda i, j: (i, j, 0)),
            const(nw.shape), const(wup.shape), const(cw.shape), const(cb.shape), const(wdn.shape),
            const(nfw.shape),
        ],
        out_specs=pl.BlockSpec((1, ts, d), lambda i, j: (i, j, 0)),
        scratch_shapes=[
            pltpu.VMEM((SUBLANES, 2 * D_FF), F32),
            pltpu.VMEM((FF_UBUFS, FF_CHUNK // LANES, SUBLANES + ts, LANES), F32),
            pltpu.VMEM((ts, D_FF), BF16),
        ],
        compiler_params=pltpu.CompilerParams(
            dimension_semantics=("parallel", "arbitrary"), vmem_limit_bytes=VMEM_LIMIT_BYTES),
        name="ffn",
    )(x, nw, wup, cw, cb, wdn, nfw)


def kernel(x, mem, norm_mix_w, w_in, pool_w, pool_scale, gk_w2, gk_b, gla_norm_w, w_out, norm_xattn_w,
           norm_mem_w, xattn_wq, xattn_wkv, xattn_wo, norm_ffn_w, ffn_w_up, ffn_conv_w, ffn_conv_b,
           ffn_w_down, norm_final_w):
    depth = w_in.shape[0]
    off_q = POOL_WIDTH
    off_v = off_q + 2 * GLA_KEY_WIDTH
    off_g = off_v + GLA_VALUE_WIDTH
    off_r = off_g + GLA_GATE_RANK
    row2 = lambda a: a.reshape(1, -1)
    for l in range(depth):
        wi = w_in[l]
        wg = jnp.pad(wi[:, off_g:off_r], ((0, 0), (0, LANES - GLA_GATE_RANK))).astype(BF16)
        gkw2 = jnp.pad(gk_w2[l], ((0, LANES - GLA_GATE_RANK), (0, 0))).astype(BF16)
        x = _mixer(
            x, row2(norm_mix_w[l]), wi[:, :off_q].astype(BF16), wi[:, off_q:off_v].astype(BF16),
            wi[:, off_v:off_g].astype(BF16), wg, wi[:, off_r:].astype(BF16), pool_w[l].astype(BF16),
            row2(pool_scale[l]), gkw2, row2(gk_b[l]), row2(gla_norm_w[l]), w_out[l].astype(BF16), ts=512)
        wkv = xattn_wkv[l]
        kt, v = _mem_kv(mem, row2(norm_mem_w[l]), wkv[:, :D_MODEL].astype(BF16), wkv[:, D_MODEL:].astype(BF16))
        x = _xattn(x, row2(norm_xattn_w[l]), xattn_wq[l].astype(BF16), kt, v, xattn_wo[l].astype(BF16), ts=512)
        assert depth == 1
        x = _ffn(x, row2(norm_ffn_w[l]), ffn_w_up[l].astype(BF16), ffn_conv_w[l], row2(ffn_conv_b[l]),
                 ffn_w_down[l].astype(BF16), row2(norm_final_w), ts=512)
    return x
```

```python
import functools

import jax
import jax.numpy as jnp
from jax import lax
from jax.experimental import pallas as pl
from jax.experimental.pallas import tpu as pltpu

D_MODEL = 1024
N_MEM = 256
POOL_WIDTH = 512
POOL_GROUPS = 4
POOL_GROUP_DIM = 128
POOL_WINDOWS = (2, 4, 8, 16)
POOL_HALO = 16
GLA_VALUE_WIDTH = 512
GLA_KEY_WIDTH = 256
GLA_HEADS = 4
GLA_DK = 64
GLA_DV = 128
GLA_GATE_RANK = 16
CUM_PAD = 32
GLA_GATE_NORMALIZER = 16.0
GLA_CHUNK = 64
XATTN_HEADS = 4
XATTN_HEAD_DIM = 256
D_FF = 2816
CONV_WIDTH = 3
EPS = 1e-6

LANES = 128
SUBLANES = 8
FF_CHUNK = 256
FF_UBUFS = 4
VMEM_LIMIT_BYTES = 56 * 1024 * 1024

BF16 = jnp.bfloat16
F32 = jnp.float32

_NT = (((1,), (1,)), ((), ()))
_TN = (((0,), (0,)), ((), ()))


def _dot(a, b):
    return jnp.dot(a, b, preferred_element_type=F32)


def _rms(x, w):
    return x * lax.rsqrt(jnp.mean(x * x, axis=-1, keepdims=True) + EPS) * w


def _sigmoid(x):
    return 1.0 / (1.0 + jnp.exp(-x))


def _mixer_kernel(x_ref, nw_ref, wp_ref, wqk_ref, wv_ref, wg_ref, wr_ref, poolw_ref, pscale_ref,
                  gkw2_ref, gkb_ref, gnw_ref, wout_ref, o_ref,
                  pbuf, gbuf, state, q_s, k_s, v_s, rs_s, cat_s, *, ts):
    s = pl.program_id(1)

    @pl.when(s == 0)
    def _():
        pbuf[:, :POOL_HALO, :] = jnp.zeros((POOL_GROUPS, POOL_HALO, LANES), F32)
        state[...] = jnp.zeros_like(state)

    x = x_ref[0]
    h = _rms(x, nw_ref[...]).astype(BF16)
    row = lax.broadcasted_iota(jnp.int32, (ts, LANES), 0)

    g_low = _dot(h, wg_ref[...])
    p = _dot(h, wp_ref[...])
    gate = _dot(g_low.astype(BF16), gkw2_ref[...]) + gkb_ref[...]
    lg = (jnp.minimum(gate, 0.0) - jnp.log1p(jnp.exp(-jnp.abs(gate)))) * (1.0 / GLA_GATE_NORMALIZER)
    crow = row % GLA_CHUNK
    for k in range(GLA_HEADS // 2):
        gb = gbuf.at[k]
        gb[:CUM_PAD, :] = jnp.zeros((CUM_PAD, LANES), F32)
        gb[CUM_PAD:, :] = lg[:, k * LANES:(k + 1) * LANES]
    shift = 1
    while shift < GLA_CHUNK:
        for k in range(GLA_HEADS // 2):
            gb = gbuf.at[k]
            cur = gb[CUM_PAD:, :]
            prev = gb[CUM_PAD - shift:CUM_PAD - shift + ts, :]
            gb[CUM_PAD:, :] = cur + jnp.where(crow >= shift, prev, 0.0)
        shift *= 2

    qk = _dot(h, wqk_ref[...])
    q_s[...] = qk[:, :GLA_KEY_WIDTH]
    k_s[...] = qk[:, GLA_KEY_WIDTH:]
    v_s[...] = _dot(h, wv_ref[...]).astype(BF16)
    r = _dot(h, wr_ref[...])
    rs_s[...] = r * _sigmoid(r)

    t1 = s * ts + row + 1
    for g, w in enumerate(POOL_WINDOWS):
        lo, hi = g * POOL_GROUP_DIM, (g + 1) * POOL_GROUP_DIM
        pg = p[:, lo:hi]
        buf = pbuf.at[g]
        buf[POOL_HALO:, :] = pg
        win = pg
        for d in range(1, w):
            win = win + buf[POOL_HALO - d:POOL_HALO - d + ts, :]
        buf[:POOL_HALO, :] = pg[ts - POOL_HALO:, :]
        cnt = jnp.minimum(t1, w).astype(F32)
        pooled = win / cnt - pg
        mixed = _dot(pooled.astype(BF16), poolw_ref[g])
        cat_s[:, lo:hi] = (mixed * pscale_ref[:, lo:hi]).astype(BF16)

    lane = lax.broadcasted_iota(jnp.int32, (1, LANES), 1)
    first_head = lane < GLA_DK
    ci = lax.broadcasted_iota(jnp.int32, (2 * GLA_CHUNK, GLA_CHUNK), 0) % GLA_CHUNK
    cj = lax.broadcasted_iota(jnp.int32, (2 * GLA_CHUNK, GLA_CHUNK), 1)
    causal2 = ci >= cj

    pairs = range(GLA_HEADS // 2)
    chunks = range(ts // GLA_CHUNK)
    qstack, scores, kvs, decays = {}, {}, {}, {}
    for c in chunks:
        rows = slice(c * GLA_CHUNK, (c + 1) * GLA_CHUNK)
        for j in pairs:
            kl = slice(j * LANES, (j + 1) * LANES)
            Gc = gbuf[j, CUM_PAD + c * GLA_CHUNK:CUM_PAD + (c + 1) * GLA_CHUNK, :]
            qc = q_s[rows, kl]
            kc = k_s[rows, kl]
            g_last = Gc[GLA_CHUNK - 1:GLA_CHUNK, :]
            q_dec = qc * (GLA_DK ** -0.5) * jnp.exp(Gc)
            k_dec = (kc * jnp.exp(-Gc)).astype(BF16)
            k_end = (kc * jnp.exp(g_last - Gc)).astype(BF16)
            decays[j, c] = jnp.exp(g_last)
            qstack[j, c] = jnp.concatenate(
                [jnp.where(first_head, q_dec, 0.0), jnp.where(first_head, 0.0, q_dec)], axis=0).astype(BF16)
            sc = lax.dot_general(qstack[j, c], k_dec, _NT, preferred_element_type=F32)
            scores[j, c] = jnp.where(causal2, sc, 0.0).astype(BF16)
            vpair = v_s[rows, 2 * j * GLA_DV:2 * (j + 1) * GLA_DV]
            kv = lax.dot_general(vpair, k_end, _TN, preferred_element_type=F32)
            kvs[j, c] = jnp.where(first_head, kv[:GLA_DV, :], kv[GLA_DV:, :])

    states = {}
    for j in pairs:
        st = state[j]
        for c in chunks:
            states[j, c] = st.astype(BF16)
            st = st * decays[j, c] + kvs[j, c]
        state[j] = st

    for c in chunks:
        rows = slice(c * GLA_CHUNK, (c + 1) * GLA_CHUNK)
        for j in pairs:
            o_inter = lax.dot_general(qstack[j, c], states[j, c], _NT, preferred_element_type=F32)
            for hh in range(2):
                hd = 2 * j + hh
                vl = slice(hd * GLA_DV, (hd + 1) * GLA_DV)
                hr = slice(hh * GLA_CHUNK, (hh + 1) * GLA_CHUNK)
                o = _dot(scores[j, c][hr, :], v_s[rows, vl]) + o_inter[hr, :]
                gl = _rms(o, gnw_ref[...]) * rs_s[rows, vl]
                cat_s[rows, POOL_WIDTH + hd * GLA_DV:POOL_WIDTH + (hd + 1) * GLA_DV] = gl.astype(BF16)

    o_ref[0] = x + _dot(cat_s[...], wout_ref[...])


def _mixer(x, nw, wp, wqk, wv, wg, wr, poolw, pscale, gkw2, gkb, gnw, wout, *, ts):
    b, s, d = x.shape
    const = lambda shape: pl.BlockSpec(shape, lambda i, j: (0,) * len(shape))
    return pl.pallas_call(
        functools.partial(_mixer_kernel, ts=ts),
        out_shape=jax.ShapeDtypeStruct(x.shape, F32),
        grid=(b, s // ts),
        in_specs=[
            pl.BlockSpec((1, ts, d), lambda i, j: (i, j, 0)),
            const(nw.shape), const(wp.shape), const(wqk.shape), const(wv.shape), const(wg.shape),
            const(wr.shape), const(poolw.shape), const(pscale.shape), const(gkw2.shape), const(gkb.shape),
            const(gnw.shape), const(wout.shape),
        ],
        out_specs=pl.BlockSpec((1, ts, d), lambda i, j: (i, j, 0)),
        scratch_shapes=[
            pltpu.VMEM((POOL_GROUPS, POOL_HALO + ts, LANES), F32),
            pltpu.VMEM((GLA_HEADS // 2, CUM_PAD + ts, LANES), F32),
            pltpu.VMEM((GLA_HEADS // 2, GLA_DV, LANES), F32),
            pltpu.VMEM((ts, GLA_KEY_WIDTH), F32),
            pltpu.VMEM((ts, GLA_KEY_WIDTH), F32),
            pltpu.VMEM((ts, GLA_VALUE_WIDTH), BF16),
            pltpu.VMEM((ts, GLA_VALUE_WIDTH), F32),
            pltpu.VMEM((ts, D_MODEL), BF16),
        ],
        compiler_params=pltpu.CompilerParams(
            dimension_semantics=("parallel", "arbitrary"), vmem_limit_bytes=VMEM_LIMIT_BYTES),
        name="mixer",
    )(x, nw, wp, wqk, wv, wg, wr, poolw, pscale, gkw2, gkb, gnw, wout)


def _mem_kv_kernel(mem_ref, nw_ref, wk_ref, wv_ref, kt_ref, v_ref):
    m = _rms(mem_ref[0], nw_ref[...]).astype(BF16)
    k = _dot(m, wk_ref[...])
    kt_ref[0] = k.T.astype(BF16)
    v_ref[0] = _dot(m, wv_ref[...]).astype(BF16)


def _mem_kv(mem, nw, wk, wv):
    b, m, d = mem.shape
    const = lambda shape: pl.BlockSpec(shape, lambda i: (0,) * len(shape))
    return pl.pallas_call(
        _mem_kv_kernel,
        out_shape=(jax.ShapeDtypeStruct((b, d, m), BF16), jax.ShapeDtypeStruct((b, m, d), BF16)),
        grid=(b,),
        in_specs=[pl.BlockSpec((1, m, d), lambda i: (i, 0, 0)), const(nw.shape), const(wk.shape), const(wv.shape)],
        out_specs=(pl.BlockSpec((1, d, m), lambda i: (i, 0, 0)), pl.BlockSpec((1, m, d), lambda i: (i, 0, 0))),
        compiler_params=pltpu.CompilerParams(
            dimension_semantics=("parallel",), vmem_limit_bytes=VMEM_LIMIT_BYTES),
        name="mem_kv",
    )(mem, nw, wk, wv)


def _xattn_kernel(x_ref, nw_ref, wq_ref, kt_ref, v_ref, wo_ref, o_ref):
    x = x_ref[0]
    h = _rms(x, nw_ref[...]).astype(BF16)
    half = D_MODEL // 2
    heads_per_half = XATTN_HEADS // 2

    def scores(g):
        cols = slice(g * half, (g + 1) * half)
        q = (_dot(h, wq_ref[:, cols]) * (XATTN_HEAD_DIM ** -0.5)).astype(BF16)
        out = []
        for i in range(heads_per_half):
            sl = slice(g * half + i * XATTN_HEAD_DIM, g * half + (i + 1) * XATTN_HEAD_DIM)
            out.append(_dot(q[:, i * XATTN_HEAD_DIM:(i + 1) * XATTN_HEAD_DIM], kt_ref[0, sl, :]))
        return out

    def attend(g, scs):
        outs = []
        for i, sc in enumerate(scs):
            sl = slice(g * half + i * XATTN_HEAD_DIM, g * half + (i + 1) * XATTN_HEAD_DIM)
            e = jnp.exp(sc - jnp.max(sc, axis=-1, keepdims=True))
            probs = e * (1.0 / jnp.sum(e, axis=-1, keepdims=True))
            outs.append(_dot(probs.astype(BF16), v_ref[0, :, sl]))
        o = jnp.concatenate(outs, axis=1).astype(BF16)
        return _dot(o, wo_ref[g * half:(g + 1) * half, :])

    sc0 = scores(0)
    sc1 = scores(1)
    o_ref[0] = x + attend(0, sc0) + attend(1, sc1)


def _xattn(x, nw, wq, kt, v, wo, *, ts):
    b, s, d = x.shape
    const = lambda shape: pl.BlockSpec(shape, lambda i, j: (0,) * len(shape))
    return pl.pallas_call(
        _xattn_kernel,
        out_shape=jax.ShapeDtypeStruct(x.shape, F32),
        grid=(b, s // ts),
        in_specs=[
            pl.BlockSpec((1, ts, d), lambda i, j: (i, j, 0)),
            const(nw.shape), const(wq.shape),
            pl.BlockSpec((1, d, N_MEM), lambda i, j: (i, 0, 0)),
            pl.BlockSpec((1, N_MEM, d), lambda i, j: (i, 0, 0)),
            const(wo.shape),
        ],
        out_specs=pl.BlockSpec((1, ts, d), lambda i, j: (i, j, 0)),
        compiler_params=pltpu.CompilerParams(
            dimension_semantics=("parallel", "parallel"), vmem_limit_bytes=VMEM_LIMIT_BYTES),
        name="xattn",
    )(x, nw, wq, kt, v, wo)


def _ffn_kernel(x_ref, nw_ref, wup_ref, cw_ref, cb_ref, wdn_ref, nfw_ref, o_ref, ucarry, ubuf, act_s, *, ts):
    s = pl.program_id(1)

    @pl.when(s == 0)
    def _():
        ucarry[...] = jnp.zeros_like(ucarry)

    x = x_ref[0]
    h = _rms(x, nw_ref[...]).astype(BF16)

    def conv_cols(col0, slot):
        u = _dot(h, wup_ref[:, col0:col0 + FF_CHUNK])
        outs = []
        for k in range(FF_CHUNK // LANES):
            cols = slice(col0 + k * LANES, col0 + (k + 1) * LANES)
            uk = u[:, k * LANES:(k + 1) * LANES]
            buf = ubuf.at[slot, k]
            buf[:SUBLANES, :] = ucarry[:, cols]
            buf[SUBLANES:, :] = uk
            ucarry[:, cols] = uk[ts - SUBLANES:, :]
            cw = cw_ref[:, cols]
            outs.append(cb_ref[:, cols] + cw[0:1, :] * buf[SUBLANES - 2:SUBLANES - 2 + ts, :]
                        + cw[1:2, :] * buf[SUBLANES - 1:SUBLANES - 1 + ts, :] + cw[2:3, :] * uk)
        return jnp.concatenate(outs, axis=1)

    for j in range(D_FF // FF_CHUNK):
        gate = conv_cols(j * FF_CHUNK, (2 * j) % FF_UBUFS)
        val = conv_cols(D_FF + j * FF_CHUNK, (2 * j + 1) % FF_UBUFS)
        act_s[:, j * FF_CHUNK:(j + 1) * FF_CHUNK] = (gate * _sigmoid(gate) * val).astype(BF16)
    o_ref[0] = _rms(x + _dot(act_s[...], wdn_ref[...]), nfw_ref[...])


def _ffn(x, nw, wup, cw, cb, wdn, nfw, *, ts):
    b, s, d = x.shape
    const = lambda shape: pl.BlockSpec(shape, lambda i, j: (0,) * len(shape))
    return pl.pallas_call(
        functools.partial(_ffn_kernel, ts=ts),
        out_shape=jax.ShapeDtypeStruct(x.shape, F32),
        grid=(b, s // ts),
        in_specs=[
            pl.BlockSpec((1, ts, d), lambda i, j: (i, j, 0)),
            const(nw.shape), const(wup.shape), const(cw.shape), const(cb.shape), const(wdn.shape),
            const(nfw.shape),
        ],
        out_specs=pl.BlockSpec((1, ts, d), lambda i, j: (i, j, 0)),
        scratch_shapes=[
            pltpu.VMEM((SUBLANES, 2 * D_FF), F32),
            pltpu.VMEM((FF_UBUFS, FF_CHUNK // LANES, SUBLANES + ts, LANES), F32),
            pltpu.VMEM((ts, D_FF), BF16),
        ],
        compiler_params=pltpu.CompilerParams(
            dimension_semantics=("parallel", "arbitrary"), vmem_limit_bytes=VMEM_LIMIT_BYTES),
        name="ffn",
    )(x, nw, wup, cw, cb, wdn, nfw)


def kernel(x, mem, norm_mix_w, w_in, pool_w, pool_scale, gk_w2, gk_b, gla_norm_w, w_out, norm_xattn_w,
           norm_mem_w, xattn_wq, xattn_wkv, xattn_wo, norm_ffn_w, ffn_w_up, ffn_conv_w, ffn_conv_b,
           ffn_w_down, norm_final_w):
    depth = w_in.shape[0]
    off_q = POOL_WIDTH
    off_v = off_q + 2 * GLA_KEY_WIDTH
    off_g = off_v + GLA_VALUE_WIDTH
    off_r = off_g + GLA_GATE_RANK
    row2 = lambda a: a.reshape(1, -1)
    for l in range(depth):
        wi = w_in[l]
        wg = jnp.pad(wi[:, off_g:off_r], ((0, 0), (0, LANES - GLA_GATE_RANK))).astype(BF16)
        gkw2 = jnp.pad(gk_w2[l], ((0, LANES - GLA_GATE_RANK), (0, 0))).astype(BF16)
        x = _mixer(
            x, row2(norm_mix_w[l]), wi[:, :off_q].astype(BF16), wi[:, off_q:off_v].astype(BF16),
            wi[:, off_v:off_g].astype(BF16), wg, wi[:, off_r:].astype(BF16), pool_w[l].astype(BF16),
            row2(pool_scale[l]), gkw2, row2(gk_b[l]), row2(gla_norm_w[l]), w_out[l].astype(BF16), ts=512)
        wkv = xattn_wkv[l]
        kt, v = _mem_kv(mem, row2(norm_mem_w[l]), wkv[:, :D_MODEL].astype(BF16), wkv[:, D_MODEL:].astype(BF16))
        x = _xattn(x, row2(norm_xattn_w[l]), xattn_wq[l].astype(BF16), kt, v, xattn_wo[l].astype(BF16), ts=512)
        assert depth == 1
        x = _ffn(x, row2(norm_ffn_w[l]), ffn_w_up[l].astype(BF16), ffn_conv_w[l], row2(ffn_conv_b[l]),
                 ffn_w_down[l].astype(BF16), row2(norm_final_w), ts=512)
    return x
```

```python
import functools

import jax
import jax.numpy as jnp
from jax import lax
from jax.experimental import pallas as pl
from jax.experimental.pallas import tpu as pltpu

D_MODEL = 1024
N_MEM = 256
POOL_WIDTH = 512
POOL_GROUPS = 4
POOL_GROUP_DIM = 128
POOL_WINDOWS = (2, 4, 8, 16)
POOL_HALO = 16
GLA_VALUE_WIDTH = 512
GLA_KEY_WIDTH = 256
GLA_HEADS = 4
GLA_DK = 64
GLA_DV = 128
GLA_GATE_RANK = 16
CUM_PAD = 32
GLA_GATE_NORMALIZER = 16.0
GLA_CHUNK = 64
XATTN_HEADS = 4
XATTN_HEAD_DIM = 256
D_FF = 2816
CONV_WIDTH = 3
EPS = 1e-6

LANES = 128
SUBLANES = 8
FF_CHUNK = 256
FF_UBUFS = 4
SEQ_TILE = 1024
VMEM_LIMIT_BYTES = 56 * 1024 * 1024

BF16 = jnp.bfloat16
F32 = jnp.float32

_NT = (((1,), (1,)), ((), ()))
_TN = (((0,), (0,)), ((), ()))


def _dot(a, b):
    return jnp.dot(a, b, preferred_element_type=F32)


def _rms(x, w):
    return x * lax.rsqrt(jnp.mean(x * x, axis=-1, keepdims=True) + EPS) * w


def _sigmoid(x):
    return 1.0 / (1.0 + jnp.exp(-x))


def _resident(shape):
    return pl.BlockSpec(shape, lambda *_: (0,) * len(shape), pipeline_mode=pl.Buffered(1))


def _mixer_kernel(x_ref, nw_ref, wp_ref, wqk_ref, wv_ref, wg_ref, wr_ref, poolw_ref, pscale_ref,
                  gkw2_ref, gkb_ref, gnw_ref, wout_ref, o_ref,
                  pbuf, gbuf, state, q_s, k_s, v_s, rs_s, cat_s, *, ts):
    s = pl.program_id(1)

    @pl.when(s == 0)
    def _():
        pbuf[:, :POOL_HALO, :] = jnp.zeros((POOL_GROUPS, POOL_HALO, LANES), F32)
        state[...] = jnp.zeros_like(state)

    x = x_ref[0]
    h = _rms(x, nw_ref[...]).astype(BF16)
    row = lax.broadcasted_iota(jnp.int32, (ts, LANES), 0)

    g_low = _dot(h, wg_ref[...])
    p = _dot(h, wp_ref[...])
    gate = _dot(g_low.astype(BF16), gkw2_ref[...]) + gkb_ref[...]
    lg = (jnp.minimum(gate, 0.0) - jnp.log1p(jnp.exp(-jnp.abs(gate)))) * (1.0 / GLA_GATE_NORMALIZER)
    crow = row % GLA_CHUNK
    for k in range(GLA_HEADS // 2):
        gb = gbuf.at[k]
        gb[:CUM_PAD, :] = jnp.zeros((CUM_PAD, LANES), F32)
        gb[CUM_PAD:, :] = lg[:, k * LANES:(k + 1) * LANES]
    shift = 1
    while shift < GLA_CHUNK:
        for k in range(GLA_HEADS // 2):
            gb = gbuf.at[k]
            cur = gb[CUM_PAD:, :]
            prev = gb[CUM_PAD - shift:CUM_PAD - shift + ts, :]
            gb[CUM_PAD:, :] = cur + jnp.where(crow >= shift, prev, 0.0)
        shift *= 2

    qk = _dot(h, wqk_ref[...])
    q_s[...] = qk[:, :GLA_KEY_WIDTH]
    k_s[...] = qk[:, GLA_KEY_WIDTH:]
    v_s[...] = _dot(h, wv_ref[...]).astype(BF16)
    r = _dot(h, wr_ref[...])
    rs_s[...] = r * _sigmoid(r)

    t1 = s * ts + row + 1
    for g, w in enumerate(POOL_WINDOWS):
        lo, hi = g * POOL_GROUP_DIM, (g + 1) * POOL_GROUP_DIM
        pg = p[:, lo:hi]
        buf = pbuf.at[g]
        buf[POOL_HALO:, :] = pg
        win = pg
        for d in range(1, w):
            win = win + buf[POOL_HALO - d:POOL_HALO - d + ts, :]
        buf[:POOL_HALO, :] = pg[ts - POOL_HALO:, :]
        cnt = jnp.minimum(t1, w).astype(F32)
        pooled = win / cnt - pg
        mixed = _dot(pooled.astype(BF16), poolw_ref[g])
        cat_s[:, lo:hi] = (mixed * pscale_ref[:, lo:hi]).astype(BF16)

    lane = lax.broadcasted_iota(jnp.int32, (1, LANES), 1)
    first_head = lane < GLA_DK
    ci = lax.broadcasted_iota(jnp.int32, (2 * GLA_CHUNK, GLA_CHUNK), 0) % GLA_CHUNK
    cj = lax.broadcasted_iota(jnp.int32, (2 * GLA_CHUNK, GLA_CHUNK), 1)
    causal2 = ci >= cj

    pairs = range(GLA_HEADS // 2)
    chunks = range(ts // GLA_CHUNK)
    qstack, scores, kvs, decays = {}, {}, {}, {}
    for c in chunks:
        rows = slice(c * GLA_CHUNK, (c + 1) * GLA_CHUNK)
        for j in pairs:
            kl = slice(j * LANES, (j + 1) * LANES)
            Gc = gbuf[j, CUM_PAD + c * GLA_CHUNK:CUM_PAD + (c + 1) * GLA_CHUNK, :]
            qc = q_s[rows, kl]
            kc = k_s[rows, kl]
            g_last = Gc[GLA_CHUNK - 1:GLA_CHUNK, :]
            q_dec = qc * (GLA_DK ** -0.5) * jnp.exp(Gc)
            k_dec = (kc * jnp.exp(-Gc)).astype(BF16)
            k_end = (kc * jnp.exp(g_last - Gc)).astype(BF16)
            decays[j, c] = jnp.exp(g_last)
            qstack[j, c] = jnp.concatenate(
                [jnp.where(first_head, q_dec, 0.0), jnp.where(first_head, 0.0, q_dec)], axis=0).astype(BF16)
            sc = lax.dot_general(qstack[j, c], k_dec, _NT, preferred_element_type=F32)
            scores[j, c] = jnp.where(causal2, sc, 0.0).astype(BF16)
            vpair = v_s[rows, 2 * j * GLA_DV:2 * (j + 1) * GLA_DV]
            kv = lax.dot_general(vpair, k_end, _TN, preferred_element_type=F32)
            kvs[j, c] = jnp.where(first_head, kv[:GLA_DV, :], kv[GLA_DV:, :])

    states = {}
    for j in pairs:
        st = state[j]
        for c in chunks:
            states[j, c] = st.astype(BF16)
            st = st * decays[j, c] + kvs[j, c]
        state[j] = st

    for c in chunks:
        rows = slice(c * GLA_CHUNK, (c + 1) * GLA_CHUNK)
        for j in pairs:
            o_inter = lax.dot_general(qstack[j, c], states[j, c], _NT, preferred_element_type=F32)
            for hh in range(2):
                hd = 2 * j + hh
                vl = slice(hd * GLA_DV, (hd + 1) * GLA_DV)
                hr = slice(hh * GLA_CHUNK, (hh + 1) * GLA_CHUNK)
                o = _dot(scores[j, c][hr, :], v_s[rows, vl]) + o_inter[hr, :]
                gl = _rms(o, gnw_ref[...]) * rs_s[rows, vl]
                cat_s[rows, POOL_WIDTH + hd * GLA_DV:POOL_WIDTH + (hd + 1) * GLA_DV] = gl.astype(BF16)

    o_ref[0] = x + _dot(cat_s[...], wout_ref[...])


def _mixer(x, nw, wp, wqk, wv, wg, wr, poolw, pscale, gkw2, gkb, gnw, wout, *, ts):
    b, s, d = x.shape
    const = _resident
    return pl.pallas_call(
        functools.partial(_mixer_kernel, ts=ts),
        out_shape=jax.ShapeDtypeStruct(x.shape, F32),
        grid=(b, s // ts),
        in_specs=[
            pl.BlockSpec((1, ts, d), lambda i, j: (i, j, 0)),
            const(nw.shape), const(wp.shape), const(wqk.shape), const(wv.shape), const(wg.shape),
            const(wr.shape), const(poolw.shape), const(pscale.shape), const(gkw2.shape), const(gkb.shape),
            const(gnw.shape), const(wout.shape),
        ],
        out_specs=pl.BlockSpec((1, ts, d), lambda i, j: (i, j, 0)),
        scratch_shapes=[
            pltpu.VMEM((POOL_GROUPS, POOL_HALO + ts, LANES), F32),
            pltpu.VMEM((GLA_HEADS // 2, CUM_PAD + ts, LANES), F32),
            pltpu.VMEM((GLA_HEADS // 2, GLA_DV, LANES), F32),
            pltpu.VMEM((ts, GLA_KEY_WIDTH), F32),
            pltpu.VMEM((ts, GLA_KEY_WIDTH), F32),
            pltpu.VMEM((ts, GLA_VALUE_WIDTH), BF16),
            pltpu.VMEM((ts, GLA_VALUE_WIDTH), F32),
            pltpu.VMEM((ts, D_MODEL), BF16),
        ],
        compiler_params=pltpu.CompilerParams(
            dimension_semantics=("parallel", "arbitrary"), vmem_limit_bytes=VMEM_LIMIT_BYTES),
        name="mixer",
    )(x, nw, wp, wqk, wv, wg, wr, poolw, pscale, gkw2, gkb, gnw, wout)


def _mem_kv_kernel(mem_ref, nw_ref, wk_ref, wv_ref, kt_ref, v_ref):
    m = _rms(mem_ref[0], nw_ref[...]).astype(BF16)
    k = _dot(m, wk_ref[...])
    kt_ref[0] = k.T.astype(BF16)
    v_ref[0] = _dot(m, wv_ref[...]).astype(BF16)


def _mem_kv(mem, nw, wk, wv):
    b, m, d = mem.shape
    const = _resident
    return pl.pallas_call(
        _mem_kv_kernel,
        out_shape=(jax.ShapeDtypeStruct((b, d, m), BF16), jax.ShapeDtypeStruct((b, m, d), BF16)),
        grid=(b,),
        in_specs=[pl.BlockSpec((1, m, d), lambda i: (i, 0, 0)), const(nw.shape), const(wk.shape), const(wv.shape)],
        out_specs=(pl.BlockSpec((1, d, m), lambda i: (i, 0, 0)), pl.BlockSpec((1, m, d), lambda i: (i, 0, 0))),
        compiler_params=pltpu.CompilerParams(
            dimension_semantics=("parallel",), vmem_limit_bytes=VMEM_LIMIT_BYTES),
        name="mem_kv",
    )(mem, nw, wk, wv)


def _xattn_kernel(x_ref, nw_ref, wq_ref, kt_ref, v_ref, wo_ref, o_ref):
    x = x_ref[0]
    h = _rms(x, nw_ref[...]).astype(BF16)
    half = D_MODEL // 2
    heads_per_half = XATTN_HEADS // 2

    def scores(g):
        cols = slice(g * half, (g + 1) * half)
        q = (_dot(h, wq_ref[:, cols]) * (XATTN_HEAD_DIM ** -0.5)).astype(BF16)
        out = []
        for i in range(heads_per_half):
            sl = slice(g * half + i * XATTN_HEAD_DIM, g * half + (i + 1) * XATTN_HEAD_DIM)
            out.append(_dot(q[:, i * XATTN_HEAD_DIM:(i + 1) * XATTN_HEAD_DIM], kt_ref[0, sl, :]))
        return out

    def attend(g, scs):
        outs = []
        for i, sc in enumerate(scs):
            sl = slice(g * half + i * XATTN_HEAD_DIM, g * half + (i + 1) * XATTN_HEAD_DIM)
            e = jnp.exp(sc - jnp.max(sc, axis=-1, keepdims=True))
            probs = e * (1.0 / jnp.sum(e, axis=-1, keepdims=True))
            outs.append(_dot(probs.astype(BF16), v_ref[0, :, sl]))
        o = jnp.concatenate(outs, axis=1).astype(BF16)
        return _dot(o, wo_ref[g * half:(g + 1) * half, :])

    sc0 = scores(0)
    sc1 = scores(1)
    o_ref[0] = x + attend(0, sc0) + attend(1, sc1)


def _xattn(x, nw, wq, kt, v, wo, *, ts):
    b, s, d = x.shape
    const = _resident
    return pl.pallas_call(
        _xattn_kernel,
        out_shape=jax.ShapeDtypeStruct(x.shape, F32),
        grid=(b, s // ts),
        in_specs=[
            pl.BlockSpec((1, ts, d), lambda i, j: (i, j, 0)),
            const(nw.shape), const(wq.shape),
            pl.BlockSpec((1, d, N_MEM), lambda i, j: (i, 0, 0)),
            pl.BlockSpec((1, N_MEM, d), lambda i, j: (i, 0, 0)),
            const(wo.shape),
        ],
        out_specs=pl.BlockSpec((1, ts, d), lambda i, j: (i, j, 0)),
        compiler_params=pltpu.CompilerParams(
            dimension_semantics=("parallel", "parallel"), vmem_limit_bytes=VMEM_LIMIT_BYTES),
        name="xattn",
    )(x, nw, wq, kt, v, wo)


def _ffn_kernel(x_ref, nw_ref, wup_ref, cw_ref, cb_ref, wdn_ref, nfw_ref, o_ref, ucarry, ubuf, act_s, *, ts):
    s = pl.program_id(1)

    @pl.when(s == 0)
    def _():
        ucarry[...] = jnp.zeros_like(ucarry)

    x = x_ref[0]
    h = _rms(x, nw_ref[...]).astype(BF16)

    def conv_cols(col0, slot):
        u = _dot(h, wup_ref[:, col0:col0 + FF_CHUNK])
        outs = []
        for k in range(FF_CHUNK // LANES):
            cols = slice(col0 + k * LANES, col0 + (k + 1) * LANES)
            uk = u[:, k * LANES:(k + 1) * LANES]
            buf = ubuf.at[slot, k]
            buf[:SUBLANES, :] = ucarry[:, cols]
            buf[SUBLANES:, :] = uk
            ucarry[:, cols] = uk[ts - SUBLANES:, :]
            cw = cw_ref[:, cols]
            outs.append(cb_ref[:, cols] + cw[0:1, :] * buf[SUBLANES - 2:SUBLANES - 2 + ts, :]
                        + cw[1:2, :] * buf[SUBLANES - 1:SUBLANES - 1 + ts, :] + cw[2:3, :] * uk)
        return jnp.concatenate(outs, axis=1)

    for j in range(D_FF // FF_CHUNK):
        gate = conv_cols(j * FF_CHUNK, (2 * j) % FF_UBUFS)
        val = conv_cols(D_FF + j * FF_CHUNK, (2 * j + 1) % FF_UBUFS)
        act_s[:, j * FF_CHUNK:(j + 1) * FF_CHUNK] = (gate * _sigmoid(gate) * val).astype(BF16)
    o_ref[0] = _rms(x + _dot(act_s[...], wdn_ref[...]), nfw_ref[...])


def _ffn(x, nw, wup, cw, cb, wdn, nfw, *, ts):
    b, s, d = x.shape
    const = _resident
    return pl.pallas_call(
        functools.partial(_ffn_kernel, ts=ts),
        out_shape=jax.ShapeDtypeStruct(x.shape, F32),
        grid=(b, s // ts),
        in_specs=[
            pl.BlockSpec((1, ts, d), lambda i, j: (i, j, 0)),
            const(nw.shape), const(wup.shape), const(cw.shape), const(cb.shape), const(wdn.shape),
            const(nfw.shape),
        ],
        out_specs=pl.BlockSpec((1, ts, d), lambda i, j: (i, j, 0)),
        scratch_shapes=[
            pltpu.VMEM((SUBLANES, 2 * D_FF), F32),
            pltpu.VMEM((FF_UBUFS, FF_CHUNK // LANES, SUBLANES + ts, LANES), F32),
            pltpu.VMEM((ts, D_FF), BF16),
        ],
        compiler_params=pltpu.CompilerParams(
            dimension_semantics=("parallel", "arbitrary"), vmem_limit_bytes=VMEM_LIMIT_BYTES),
        name="ffn",
    )(x, nw, wup, cw, cb, wdn, nfw)


def kernel(x, mem, norm_mix_w, w_in, pool_w, pool_scale, gk_w2, gk_b, gla_norm_w, w_out, norm_xattn_w,
           norm_mem_w, xattn_wq, xattn_wkv, xattn_wo, norm_ffn_w, ffn_w_up, ffn_conv_w, ffn_conv_b,
           ffn_w_down, norm_final_w):
    depth = w_in.shape[0]
    off_q = POOL_WIDTH
    off_v = off_q + 2 * GLA_KEY_WIDTH
    off_g = off_v + GLA_VALUE_WIDTH
    off_r = off_g + GLA_GATE_RANK
    row2 = lambda a: a.reshape(1, -1)
    for l in range(depth):
        wi = w_in[l]
        wg = jnp.pad(wi[:, off_g:off_r], ((0, 0), (0, LANES - GLA_GATE_RANK))).astype(BF16)
        gkw2 = jnp.pad(gk_w2[l], ((0, LANES - GLA_GATE_RANK), (0, 0))).astype(BF16)
        x = _mixer(
            x, row2(norm_mix_w[l]), wi[:, :off_q].astype(BF16), wi[:, off_q:off_v].astype(BF16),
            wi[:, off_v:off_g].astype(BF16), wg, wi[:, off_r:].astype(BF16), pool_w[l].astype(BF16),
            row2(pool_scale[l]), gkw2, row2(gk_b[l]), row2(gla_norm_w[l]), w_out[l].astype(BF16), ts=SEQ_TILE)
        wkv = xattn_wkv[l]
        kt, v = _mem_kv(mem, row2(norm_mem_w[l]), wkv[:, :D_MODEL].astype(BF16), wkv[:, D_MODEL:].astype(BF16))
        x = _xattn(x, row2(norm_xattn_w[l]), xattn_wq[l].astype(BF16), kt, v, xattn_wo[l].astype(BF16), ts=SEQ_TILE)
        assert depth == 1
        x = _ffn(x, row2(norm_ffn_w[l]), ffn_w_up[l].astype(BF16), ffn_conv_w[l], row2(ffn_conv_b[l]),
                 ffn_w_down[l].astype(BF16), row2(norm_final_w), ts=SEQ_TILE)
    return x
```

```python
import functools

import jax
import jax.numpy as jnp
from jax import lax
from jax.experimental import pallas as pl
from jax.experimental.pallas import tpu as pltpu

D_MODEL = 1024
N_MEM = 256
POOL_WIDTH = 512
POOL_GROUPS = 4
POOL_GROUP_DIM = 128
POOL_WINDOWS = (2, 4, 8, 16)
POOL_HALO = 16
GLA_VALUE_WIDTH = 512
GLA_KEY_WIDTH = 256
GLA_HEADS = 4
GLA_DK = 64
GLA_DV = 128
GLA_GATE_RANK = 16
CUM_PAD = 32
GLA_GATE_NORMALIZER = 16.0
GLA_CHUNK = 64
XATTN_HEADS = 4
XATTN_HEAD_DIM = 256
D_FF = 2816
CONV_WIDTH = 3
EPS = 1e-6

LANES = 128
SUBLANES = 8
FF_CHUNK = 256
FF_UBUFS = 4
SEQ_TILE = 1024
SUB_TILE = 512
VMEM_LIMIT_BYTES = 56 * 1024 * 1024

BF16 = jnp.bfloat16
F32 = jnp.float32

_NT = (((1,), (1,)), ((), ()))
_TN = (((0,), (0,)), ((), ()))


def _dot(a, b):
    return jnp.dot(a, b, preferred_element_type=F32)


def _rms(x, w):
    return x * lax.rsqrt(jnp.mean(x * x, axis=-1, keepdims=True) + EPS) * w


def _sigmoid(x):
    return 1.0 / (1.0 + jnp.exp(-x))


def _resident(shape):
    return pl.BlockSpec(shape, lambda *_: (0,) * len(shape), pipeline_mode=pl.Buffered(1))


def _mixer_kernel(x_ref, nw_ref, wp_ref, wqk_ref, wv_ref, wg_ref, wr_ref, poolw_ref, pscale_ref,
                  gkw2_ref, gkb_ref, gnw_ref, wout_ref, o_ref,
                  pbuf, gbuf, state, q_s, k_s, v_s, rs_s, cat_s, *, ts, sub):
    s = pl.program_id(1)

    @pl.when(s == 0)
    def _():
        pbuf[:, :POOL_HALO, :] = jnp.zeros((POOL_GROUPS, POOL_HALO, LANES), F32)
        state[...] = jnp.zeros_like(state)

    row = lax.broadcasted_iota(jnp.int32, (sub, LANES), 0)
    crow = row % GLA_CHUNK
    lane = lax.broadcasted_iota(jnp.int32, (1, LANES), 1)
    first_head = lane < GLA_DK
    ci = lax.broadcasted_iota(jnp.int32, (GLA_CHUNK, LANES), 0)
    cj = lax.broadcasted_iota(jnp.int32, (GLA_CHUNK, LANES), 1) % GLA_CHUNK
    causal2 = ci >= cj
    pairs = range(GLA_HEADS // 2)

    def by_head(a):
        return jnp.concatenate([jnp.where(first_head, a, 0.0), jnp.where(first_head, 0.0, a)], axis=0)

    def project(a):
        rows = slice(a, a + sub)
        h = _rms(x_ref[0, rows, :], nw_ref[...]).astype(BF16)
        g_low = _dot(h, wg_ref[...])
        p = _dot(h, wp_ref[...])
        gate = _dot(g_low.astype(BF16), gkw2_ref[...]) + gkb_ref[...]
        lg = (jnp.minimum(gate, 0.0) - jnp.log1p(jnp.exp(-jnp.abs(gate)))) * (1.0 / GLA_GATE_NORMALIZER)
        g0 = CUM_PAD + a
        for k in pairs:
            gb = gbuf.at[k]
            if a == 0:
                gb[:CUM_PAD, :] = jnp.zeros((CUM_PAD, LANES), F32)
            gb[g0:g0 + sub, :] = lg[:, k * LANES:(k + 1) * LANES]
        shift = 1
        while shift < GLA_CHUNK:
            for k in pairs:
                gb = gbuf.at[k]
                cur = gb[g0:g0 + sub, :]
                prev = gb[g0 - shift:g0 - shift + sub, :]
                gb[g0:g0 + sub, :] = cur + jnp.where(crow >= shift, prev, 0.0)
            shift *= 2

        qk = _dot(h, wqk_ref[...])
        q_s[rows, :] = qk[:, :GLA_KEY_WIDTH]
        k_s[rows, :] = qk[:, GLA_KEY_WIDTH:]
        v_s[rows, :] = _dot(h, wv_ref[...]).astype(BF16)
        r = _dot(h, wr_ref[...])
        rs_s[rows, :] = r * _sigmoid(r)

        t1 = s * ts + a + row + 1
        p0 = POOL_HALO + a
        for g, w in enumerate(POOL_WINDOWS):
            lo, hi = g * POOL_GROUP_DIM, (g + 1) * POOL_GROUP_DIM
            pg = p[:, lo:hi]
            buf = pbuf.at[g]
            buf[p0:p0 + sub, :] = pg
            win = pg
            for d in range(1, w):
                win = win + buf[p0 - d:p0 - d + sub, :]
            if a + sub == ts:
                buf[:POOL_HALO, :] = pg[sub - POOL_HALO:, :]
            cnt = jnp.minimum(t1, w).astype(F32)
            pooled = win / cnt - pg
            mixed = _dot(pooled.astype(BF16), poolw_ref[g])
            cat_s[rows, lo:hi] = (mixed * pscale_ref[:, lo:hi]).astype(BF16)

    def gla_products(a):
        out = {}
        for c in range(sub // GLA_CHUNK):
            r0 = a + c * GLA_CHUNK
            rows = slice(r0, r0 + GLA_CHUNK)
            for j in pairs:
                kl = slice(j * LANES, (j + 1) * LANES)
                Gc = gbuf[j, CUM_PAD + r0:CUM_PAD + r0 + GLA_CHUNK, :]
                qc = q_s[rows, kl]
                kc = k_s[rows, kl]
                g_last = Gc[GLA_CHUNK - 1:GLA_CHUNK, :]
                q_dec = qc * (GLA_DK ** -0.5) * jnp.exp(Gc)
                k_dec = by_head(kc * jnp.exp(-Gc)).astype(BF16)
                k_end = by_head(kc * jnp.exp(g_last - Gc)).astype(BF16)
                sc = lax.dot_general(q_dec.astype(BF16), k_dec, _NT, preferred_element_type=F32)
                v_rows = jnp.concatenate(
                    [v_s[rows, (2 * j + hh) * GLA_DV:(2 * j + hh + 1) * GLA_DV] for hh in range(2)], axis=0)
                out[j, c] = dict(
                    decay=jnp.exp(g_last),
                    q_rows=by_head(q_dec).astype(BF16),
                    sc_rows=by_head(jnp.where(causal2, sc, 0.0)).astype(BF16),
                    v_rows=v_rows,
                    kv=lax.dot_general(v_rows, k_end, _TN, preferred_element_type=F32))
        return out

    def gla_states(prod, st):
        st = list(st)
        for c in range(sub // GLA_CHUNK):
            for j in pairs:
                prod[j, c]["state"] = st[j]
                st[j] = st[j] * prod[j, c]["decay"] + prod[j, c]["kv"]
        return st

    def gla_outputs(a, prod):
        for c in range(sub // GLA_CHUNK):
            r0 = a + c * GLA_CHUNK
            rows = slice(r0, r0 + GLA_CHUNK)
            for j in pairs:
                pc = prod[j, c]
                lhs = jnp.concatenate([pc["sc_rows"], pc["q_rows"]], axis=1)
                rhs = jnp.concatenate([pc["v_rows"], pc["state"].T.astype(BF16)], axis=0)
                o2 = _dot(lhs, rhs)
                for hh in range(2):
                    hd = 2 * j + hh
                    vl = slice(hd * GLA_DV, (hd + 1) * GLA_DV)
                    o = o2[hh * GLA_CHUNK:(hh + 1) * GLA_CHUNK, :]
                    gl = _rms(o, gnw_ref[...]) * rs_s[rows, vl]
                    cat_s[rows, POOL_WIDTH + hd * GLA_DV:POOL_WIDTH + (hd + 1) * GLA_DV] = gl.astype(BF16)

    def output(a):
        rows = slice(a, a + sub)
        o_ref[0, rows, :] = x_ref[0, rows, :] + _dot(cat_s[rows, :], wout_ref[...])

    starts = list(range(0, ts, sub))
    st = [state[j] for j in pairs]
    project(starts[0])
    if len(starts) > 1:
        project(starts[1])
    prod = gla_products(starts[0])
    st = gla_states(prod, st)
    for i, a in enumerate(starts):
        gla_outputs(a, prod)
        if i + 1 < len(starts):
            if i + 2 < len(starts):
                project(starts[i + 2])
            prod = gla_products(starts[i + 1])
            st = gla_states(prod, st)
        output(a)
    for j in pairs:
        state[j] = st[j]


def _mixer(x, nw, wp, wqk, wv, wg, wr, poolw, pscale, gkw2, gkb, gnw, wout, *, ts):
    b, s, d = x.shape
    const = _resident
    return pl.pallas_call(
        functools.partial(_mixer_kernel, ts=ts, sub=min(ts, SUB_TILE)),
        out_shape=jax.ShapeDtypeStruct(x.shape, F32),
        grid=(b, s // ts),
        in_specs=[
            pl.BlockSpec((1, ts, d), lambda i, j: (i, j, 0)),
            const(nw.shape), const(wp.shape), const(wqk.shape), const(wv.shape), const(wg.shape),
            const(wr.shape), const(poolw.shape), const(pscale.shape), const(gkw2.shape), const(gkb.shape),
            const(gnw.shape), const(wout.shape),
        ],
        out_specs=pl.BlockSpec((1, ts, d), lambda i, j: (i, j, 0)),
        scratch_shapes=[
            pltpu.VMEM((POOL_GROUPS, POOL_HALO + ts, LANES), F32),
            pltpu.VMEM((GLA_HEADS // 2, CUM_PAD + ts, LANES), F32),
            pltpu.VMEM((GLA_HEADS // 2, GLA_DV, LANES), F32),
            pltpu.VMEM((ts, GLA_KEY_WIDTH), F32),
            pltpu.VMEM((ts, GLA_KEY_WIDTH), F32),
            pltpu.VMEM((ts, GLA_VALUE_WIDTH), BF16),
            pltpu.VMEM((ts, GLA_VALUE_WIDTH), F32),
            pltpu.VMEM((ts, D_MODEL), BF16),
        ],
        compiler_params=pltpu.CompilerParams(
            dimension_semantics=("parallel", "arbitrary"), vmem_limit_bytes=VMEM_LIMIT_BYTES),
        name="mixer",
    )(x, nw, wp, wqk, wv, wg, wr, poolw, pscale, gkw2, gkb, gnw, wout)


def _mem_kv_kernel(mem_ref, nw_ref, wk_ref, wv_ref, kt_ref, v_ref):
    m = _rms(mem_ref[0], nw_ref[...]).astype(BF16)
    k = _dot(m, wk_ref[...])
    kt_ref[0] = k.T.astype(BF16)
    v_ref[0] = _dot(m, wv_ref[...]).astype(BF16)


def _mem_kv(mem, nw, wk, wv):
    b, m, d = mem.shape
    const = _resident
    return pl.pallas_call(
        _mem_kv_kernel,
        out_shape=(jax.ShapeDtypeStruct((b, d, m), BF16), jax.ShapeDtypeStruct((b, m, d), BF16)),
        grid=(b,),
        in_specs=[pl.BlockSpec((1, m, d), lambda i: (i, 0, 0)), const(nw.shape), const(wk.shape), const(wv.shape)],
        out_specs=(pl.BlockSpec((1, d, m), lambda i: (i, 0, 0)), pl.BlockSpec((1, m, d), lambda i: (i, 0, 0))),
        compiler_params=pltpu.CompilerParams(
            dimension_semantics=("parallel",), vmem_limit_bytes=VMEM_LIMIT_BYTES),
        name="mem_kv",
    )(mem, nw, wk, wv)


def _xattn_kernel(x_ref, nw_ref, wq_ref, kt_ref, v_ref, wo_ref, o_ref, *, ts, sub):
    half = D_MODEL // 2
    heads_per_half = XATTN_HEADS // 2

    def scores(a):
        h = _rms(x_ref[0, a:a + sub, :], nw_ref[...]).astype(BF16)
        out = []
        for g in range(2):
            q = (_dot(h, wq_ref[:, g * half:(g + 1) * half]) * (XATTN_HEAD_DIM ** -0.5)).astype(BF16)
            for i in range(heads_per_half):
                sl = slice(g * half + i * XATTN_HEAD_DIM, g * half + (i + 1) * XATTN_HEAD_DIM)
                out.append(_dot(q[:, i * XATTN_HEAD_DIM:(i + 1) * XATTN_HEAD_DIM], kt_ref[0, sl, :]))
        return out

    def attend(a, scs):
        y = x_ref[0, a:a + sub, :]
        for g in range(2):
            outs = []
            for i in range(heads_per_half):
                sc = scs[g * heads_per_half + i]
                sl = slice(g * half + i * XATTN_HEAD_DIM, g * half + (i + 1) * XATTN_HEAD_DIM)
                e = jnp.exp(sc - jnp.max(sc, axis=-1, keepdims=True))
                probs = e * (1.0 / jnp.sum(e, axis=-1, keepdims=True))
                outs.append(_dot(probs.astype(BF16), v_ref[0, :, sl]))
            o = jnp.concatenate(outs, axis=1).astype(BF16)
            y = y + _dot(o, wo_ref[g * half:(g + 1) * half, :])
        o_ref[0, a:a + sub, :] = y

    starts = list(range(0, ts, sub))
    scs = scores(starts[0])
    for i, a in enumerate(starts):
        nxt = scores(starts[i + 1]) if i + 1 < len(starts) else None
        attend(a, scs)
        scs = nxt


def _xattn(x, nw, wq, kt, v, wo, *, ts):
    b, s, d = x.shape
    const = _resident
    return pl.pallas_call(
        functools.partial(_xattn_kernel, ts=ts, sub=min(ts, SUB_TILE)),
        out_shape=jax.ShapeDtypeStruct(x.shape, F32),
        grid=(b, s // ts),
        in_specs=[
            pl.BlockSpec((1, ts, d), lambda i, j: (i, j, 0)),
            const(nw.shape), const(wq.shape),
            pl.BlockSpec((1, d, N_MEM), lambda i, j: (i, 0, 0)),
            pl.BlockSpec((1, N_MEM, d), lambda i, j: (i, 0, 0)),
            const(wo.shape),
        ],
        out_specs=pl.BlockSpec((1, ts, d), lambda i, j: (i, j, 0)),
        compiler_params=pltpu.CompilerParams(
            dimension_semantics=("parallel", "parallel"), vmem_limit_bytes=VMEM_LIMIT_BYTES),
        name="xattn",
    )(x, nw, wq, kt, v, wo)


def _ffn_kernel(x_ref, nw_ref, wup_ref, cw_ref, cb_ref, wdn_ref, nfw_ref, o_ref, ucarry, ubuf, act_s, *, ts):
    s = pl.program_id(1)

    @pl.when(s == 0)
    def _():
        ucarry[...] = jnp.zeros_like(ucarry)

    x = x_ref[0]
    h = _rms(x, nw_ref[...]).astype(BF16)

    def conv_cols(col0, slot):
        u = _dot(h, wup_ref[:, col0:col0 + FF_CHUNK])
        outs = []
        for k in range(FF_CHUNK // LANES):
            cols = slice(col0 + k * LANES, col0 + (k + 1) * LANES)
            uk = u[:, k * LANES:(k + 1) * LANES]
            buf = ubuf.at[slot, k]
            buf[:SUBLANES, :] = ucarry[:, cols]
            buf[SUBLANES:, :] = uk
            ucarry[:, cols] = uk[ts - SUBLANES:, :]
            cw = cw_ref[:, cols]
            outs.append(cb_ref[:, cols] + cw[0:1, :] * buf[SUBLANES - 2:SUBLANES - 2 + ts, :]
                        + cw[1:2, :] * buf[SUBLANES - 1:SUBLANES - 1 + ts, :] + cw[2:3, :] * uk)
        return jnp.concatenate(outs, axis=1)

    for j in range(D_FF // FF_CHUNK):
        gate = conv_cols(j * FF_CHUNK, (2 * j) % FF_UBUFS)
        val = conv_cols(D_FF + j * FF_CHUNK, (2 * j + 1) % FF_UBUFS)
        act_s[:, j * FF_CHUNK:(j + 1) * FF_CHUNK] = (gate * _sigmoid(gate) * val).astype(BF16)
    o_ref[0] = _rms(x + _dot(act_s[...], wdn_ref[...]), nfw_ref[...])


def _ffn(x, nw, wup, cw, cb, wdn, nfw, *, ts):
    b, s, d = x.shape
    const = _resident
    return pl.pallas_call(
        functools.partial(_ffn_kernel, ts=ts),
        out_shape=jax.ShapeDtypeStruct(x.shape, F32),
        grid=(b, s // ts),
        in_specs=[
            pl.BlockSpec((1, ts, d), lambda i, j: (i, j, 0)),
            const(nw.shape), const(wup.shape), const(cw.shape), const(cb.shape), const(wdn.shape),
            const(nfw.shape),
        ],
        out_specs=pl.BlockSpec((1, ts, d), lambda i, j: (i, j, 0)),
        scratch_shapes=[
            pltpu.VMEM((SUBLANES, 2 * D_FF), F32),
            pltpu.VMEM((FF_UBUFS, FF_CHUNK // LANES, SUBLANES + ts, LANES), F32),
            pltpu.VMEM((ts, D_FF), BF16),
        ],
        compiler_params=pltpu.CompilerParams(
            dimension_semantics=("parallel", "arbitrary"), vmem_limit_bytes=VMEM_LIMIT_BYTES),
        name="ffn",
    )(x, nw, wup, cw, cb, wdn, nfw)


def kernel(x, mem, norm_mix_w, w_in, pool_w, pool_scale, gk_w2, gk_b, gla_norm_w, w_out, norm_xattn_w,
           norm_mem_w, xattn_wq, xattn_wkv, xattn_wo, norm_ffn_w, ffn_w_up, ffn_conv_w, ffn_conv_b,
           ffn_w_down, norm_final_w):
    depth = w_in.shape[0]
    off_q = POOL_WIDTH
    off_v = off_q + 2 * GLA_KEY_WIDTH
    off_g = off_v + GLA_VALUE_WIDTH
    off_r = off_g + GLA_GATE_RANK
    row2 = lambda a: a.reshape(1, -1)
    for l in range(depth):
        wi = w_in[l]
        wg = jnp.pad(wi[:, off_g:off_r], ((0, 0), (0, LANES - GLA_GATE_RANK))).astype(BF16)
        gkw2 = jnp.pad(gk_w2[l], ((0, LANES - GLA_GATE_RANK), (0, 0))).astype(BF16)
        x = _mixer(
            x, row2(norm_mix_w[l]), wi[:, :off_q].astype(BF16), wi[:, off_q:off_v].astype(BF16),
            wi[:, off_v:off_g].astype(BF16), wg, wi[:, off_r:].astype(BF16), pool_w[l].astype(BF16),
            row2(pool_scale[l]), gkw2, row2(gk_b[l]), row2(gla_norm_w[l]), w_out[l].astype(BF16), ts=SEQ_TILE)
        wkv = xattn_wkv[l]
        kt, v = _mem_kv(mem, row2(norm_mem_w[l]), wkv[:, :D_MODEL].astype(BF16), wkv[:, D_MODEL:].astype(BF16))
        x = _xattn(x, row2(norm_xattn_w[l]), xattn_wq[l].astype(BF16), kt, v, xattn_wo[l].astype(BF16), ts=SEQ_TILE)
        assert depth == 1
        x = _ffn(x, row2(norm_ffn_w[l]), ffn_w_up[l].astype(BF16), ffn_conv_w[l], row2(ffn_conv_b[l]),
                 ffn_w_down[l].astype(BF16), row2(norm_final_w), ts=SEQ_TILE)
    return x
```

```python
import functools

import jax
import jax.numpy as jnp
from jax import lax
from jax.experimental import pallas as pl
from jax.experimental.pallas import tpu as pltpu

D_MODEL = 1024
N_MEM = 256
POOL_WIDTH = 512
POOL_GROUPS = 4
POOL_GROUP_DIM = 128
POOL_WINDOWS = (2, 4, 8, 16)
POOL_HALO = 16
GLA_VALUE_WIDTH = 512
GLA_KEY_WIDTH = 256
GLA_HEADS = 4
GLA_DK = 64
GLA_DV = 128
GLA_GATE_RANK = 16
CUM_PAD = 32
GLA_GATE_NORMALIZER = 16.0
GLA_CHUNK = 64
XATTN_HEADS = 4
XATTN_HEAD_DIM = 256
D_FF = 2816
CONV_WIDTH = 3
EPS = 1e-6

LANES = 128
SUBLANES = 8
FF_CHUNK = 256
FF_UBUFS = 4
SEQ_TILE = 1024
SUB_TILE = 512
OUT_COLS = 256
PROJECT_FILLS = 5
VMEM_LIMIT_BYTES = 56 * 1024 * 1024

BF16 = jnp.bfloat16
F32 = jnp.float32

_NT = (((1,), (1,)), ((), ()))
_TN = (((0,), (0,)), ((), ()))


def _dot(a, b):
    return jnp.dot(a, b, preferred_element_type=F32)


def _rms(x, w):
    return x * lax.rsqrt(jnp.mean(x * x, axis=-1, keepdims=True) + EPS) * w


def _sigmoid(x):
    return 1.0 / (1.0 + jnp.exp(-x))


def _resident(shape):
    return pl.BlockSpec(shape, lambda *_: (0,) * len(shape), pipeline_mode=pl.Buffered(1))


def _mixer_kernel(x_ref, nw_ref, wp_ref, wqk_ref, wv_ref, wg_ref, wr_ref, poolw_ref, pscale_ref,
                  gkw2_ref, gkb_ref, gnw_ref, wout_ref, o_ref,
                  pbuf, gbuf, state, q_s, k_s, v_s, rs_s, cat_s, *, ts, sub):
    s = pl.program_id(1)

    @pl.when(s == 0)
    def _():
        pbuf[:, :POOL_HALO, :] = jnp.zeros((POOL_GROUPS, POOL_HALO, LANES), F32)
        state[...] = jnp.zeros_like(state)

    row = lax.broadcasted_iota(jnp.int32, (sub, LANES), 0)
    crow = row % GLA_CHUNK
    lane = lax.broadcasted_iota(jnp.int32, (1, LANES), 1)
    first_head = lane < GLA_DK
    ci = lax.broadcasted_iota(jnp.int32, (GLA_CHUNK, LANES), 0)
    cj = lax.broadcasted_iota(jnp.int32, (GLA_CHUNK, LANES), 1) % GLA_CHUNK
    causal2 = ci >= cj
    pairs = range(GLA_HEADS // 2)

    def by_head(a):
        return jnp.concatenate([jnp.where(first_head, a, 0.0), jnp.where(first_head, 0.0, a)], axis=0)

    def project(a, fill=lambda: None):
        rows = slice(a, a + sub)
        h = _rms(x_ref[0, rows, :], nw_ref[...]).astype(BF16)
        g_low = _dot(h, wg_ref[...])
        p = _dot(h, wp_ref[...])
        fill()
        gate = _dot(g_low.astype(BF16), gkw2_ref[...]) + gkb_ref[...]
        lg = (jnp.minimum(gate, 0.0) - jnp.log1p(jnp.exp(-jnp.abs(gate)))) * (1.0 / GLA_GATE_NORMALIZER)
        g0 = CUM_PAD + a
        for k in pairs:
            gb = gbuf.at[k]
            if a == 0:
                gb[:CUM_PAD, :] = jnp.zeros((CUM_PAD, LANES), F32)
            gb[g0:g0 + sub, :] = lg[:, k * LANES:(k + 1) * LANES]
        shift = 1
        while shift < GLA_CHUNK:
            for k in pairs:
                gb = gbuf.at[k]
                cur = gb[g0:g0 + sub, :]
                prev = gb[g0 - shift:g0 - shift + sub, :]
                gb[g0:g0 + sub, :] = cur + jnp.where(crow >= shift, prev, 0.0)
            shift *= 2

        qk = _dot(h, wqk_ref[...])
        q_s[rows, :] = qk[:, :GLA_KEY_WIDTH]
        k_s[rows, :] = qk[:, GLA_KEY_WIDTH:]
        fill()
        v_s[rows, :] = _dot(h, wv_ref[...]).astype(BF16)
        fill()
        r = _dot(h, wr_ref[...])
        rs_s[rows, :] = r * _sigmoid(r)
        fill()

        t1 = s * ts + a + row + 1
        p0 = POOL_HALO + a
        for g, w in enumerate(POOL_WINDOWS):
            lo, hi = g * POOL_GROUP_DIM, (g + 1) * POOL_GROUP_DIM
            pg = p[:, lo:hi]
            buf = pbuf.at[g]
            buf[p0:p0 + sub, :] = pg
            win = pg
            for d in range(1, w):
                win = win + buf[p0 - d:p0 - d + sub, :]
            if a + sub == ts:
                buf[:POOL_HALO, :] = pg[sub - POOL_HALO:, :]
            cnt = jnp.minimum(t1, w).astype(F32)
            pooled = win / cnt - pg
            mixed = _dot(pooled.astype(BF16), poolw_ref[g])
            cat_s[rows, lo:hi] = (mixed * pscale_ref[:, lo:hi]).astype(BF16)
        fill()

    prod = {}
    st = [state[j] for j in pairs]

    def gla_products(r0):
        rows = slice(r0, r0 + GLA_CHUNK)
        for j in pairs:
            kl = slice(j * LANES, (j + 1) * LANES)
            Gc = gbuf[j, CUM_PAD + r0:CUM_PAD + r0 + GLA_CHUNK, :]
            qc = q_s[rows, kl]
            kc = k_s[rows, kl]
            g_last = Gc[GLA_CHUNK - 1:GLA_CHUNK, :]
            q_dec = qc * (GLA_DK ** -0.5) * jnp.exp(Gc)
            k_dec = by_head(kc * jnp.exp(-Gc)).astype(BF16)
            k_end = by_head(kc * jnp.exp(g_last - Gc)).astype(BF16)
            sc = lax.dot_general(q_dec.astype(BF16), k_dec, _NT, preferred_element_type=F32)
            v_rows = jnp.concatenate(
                [v_s[rows, (2 * j + hh) * GLA_DV:(2 * j + hh + 1) * GLA_DV] for hh in range(2)], axis=0)
            prod[r0, j] = dict(
                decay=jnp.exp(g_last),
                q_rows=by_head(q_dec).astype(BF16),
                sc_rows=by_head(jnp.where(causal2, sc, 0.0)).astype(BF16),
                v_rows=v_rows,
                kv=lax.dot_general(v_rows, k_end, _TN, preferred_element_type=F32))

    def gla_outputs(r0):
        rows = slice(r0, r0 + GLA_CHUNK)
        for j in pairs:
            pc = prod.pop((r0, j))
            lhs = jnp.concatenate([pc["sc_rows"], pc["q_rows"]], axis=1)
            rhs = jnp.concatenate([pc["v_rows"], st[j].T.astype(BF16)], axis=0)
            st[j] = st[j] * pc["decay"] + pc["kv"]
            o2 = _dot(lhs, rhs)
            for hh in range(2):
                hd = 2 * j + hh
                vl = slice(hd * GLA_DV, (hd + 1) * GLA_DV)
                o = o2[hh * GLA_CHUNK:(hh + 1) * GLA_CHUNK, :]
                gl = _rms(o, gnw_ref[...]) * rs_s[rows, vl]
                cat_s[rows, POOL_WIDTH + hd * GLA_DV:POOL_WIDTH + (hd + 1) * GLA_DV] = gl.astype(BF16)

    def output(a, fill=lambda: None):
        rows = slice(a, a + sub)
        for n in range(0, D_MODEL, OUT_COLS):
            fill()
            cols = slice(n, n + OUT_COLS)
            o_ref[0, rows, cols] = x_ref[0, rows, cols] + _dot(cat_s[rows, :], wout_ref[:, cols])

    def small_work(a):
        chunk_rows = range(a, a + sub, GLA_CHUNK)
        for r0 in chunk_rows:
            gla_products(r0)
            yield
        for r0 in chunk_rows:
            gla_outputs(r0)
            yield

    def filler(work, pieces):
        def fill():
            for _ in range(pieces):
                next(work, None)
        return fill

    starts = list(range(0, ts, sub))
    n_small = 2 * (sub // GLA_CHUNK)
    project(starts[0])
    work = small_work(starts[0])
    for i, a in enumerate(starts):
        if i + 1 < len(starts):
            project(starts[i + 1], filler(work, -(-n_small // PROJECT_FILLS)))
        for _ in work:
            pass
        if i + 1 < len(starts):
            work = small_work(starts[i + 1])
            output(a, filler(work, -(-n_small // (D_MODEL // OUT_COLS))))
        else:
            output(a)
    for j in pairs:
        state[j] = st[j]


def _mixer(x, nw, wp, wqk, wv, wg, wr, poolw, pscale, gkw2, gkb, gnw, wout, *, ts):
    b, s, d = x.shape
    const = _resident
    return pl.pallas_call(
        functools.partial(_mixer_kernel, ts=ts, sub=min(ts, SUB_TILE)),
        out_shape=jax.ShapeDtypeStruct(x.shape, F32),
        grid=(b, s // ts),
        in_specs=[
            pl.BlockSpec((1, ts, d), lambda i, j: (i, j, 0)),
            const(nw.shape), const(wp.shape), const(wqk.shape), const(wv.shape), const(wg.shape),
            const(wr.shape), const(poolw.shape), const(pscale.shape), const(gkw2.shape), const(gkb.shape),
            const(gnw.shape), const(wout.shape),
        ],
        out_specs=pl.BlockSpec((1, ts, d), lambda i, j: (i, j, 0)),
        scratch_shapes=[
            pltpu.VMEM((POOL_GROUPS, POOL_HALO + ts, LANES), F32),
            pltpu.VMEM((GLA_HEADS // 2, CUM_PAD + ts, LANES), F32),
            pltpu.VMEM((GLA_HEADS // 2, GLA_DV, LANES), F32),
            pltpu.VMEM((ts, GLA_KEY_WIDTH), F32),
            pltpu.VMEM((ts, GLA_KEY_WIDTH), F32),
            pltpu.VMEM((ts, GLA_VALUE_WIDTH), BF16),
            pltpu.VMEM((ts, GLA_VALUE_WIDTH), F32),
            pltpu.VMEM((ts, D_MODEL), BF16),
        ],
        compiler_params=pltpu.CompilerParams(
            dimension_semantics=("parallel", "arbitrary"), vmem_limit_bytes=VMEM_LIMIT_BYTES),
        name="mixer",
    )(x, nw, wp, wqk, wv, wg, wr, poolw, pscale, gkw2, gkb, gnw, wout)


def _mem_kv_kernel(mem_ref, nw_ref, wk_ref, wv_ref, kt_ref, v_ref):
    m = _rms(mem_ref[0], nw_ref[...]).astype(BF16)
    k = _dot(m, wk_ref[...])
    kt_ref[0] = k.T.astype(BF16)
    v_ref[0] = _dot(m, wv_ref[...]).astype(BF16)


def _mem_kv(mem, nw, wk, wv):
    b, m, d = mem.shape
    const = _resident
    return pl.pallas_call(
        _mem_kv_kernel,
        out_shape=(jax.ShapeDtypeStruct((b, d, m), BF16), jax.ShapeDtypeStruct((b, m, d), BF16)),
        grid=(b,),
        in_specs=[pl.BlockSpec((1, m, d), lambda i: (i, 0, 0)), const(nw.shape), const(wk.shape), const(wv.shape)],
        out_specs=(pl.BlockSpec((1, d, m), lambda i: (i, 0, 0)), pl.BlockSpec((1, m, d), lambda i: (i, 0, 0))),
        compiler_params=pltpu.CompilerParams(
            dimension_semantics=("parallel",), vmem_limit_bytes=VMEM_LIMIT_BYTES),
        name="mem_kv",
    )(mem, nw, wk, wv)


def _xattn_kernel(x_ref, nw_ref, wq_ref, kt_ref, v_ref, wo_ref, o_ref, *, ts, sub):
    half = D_MODEL // 2
    heads_per_half = XATTN_HEADS // 2

    def scores(a):
        h = _rms(x_ref[0, a:a + sub, :], nw_ref[...]).astype(BF16)
        out = []
        for g in range(2):
            q = (_dot(h, wq_ref[:, g * half:(g + 1) * half]) * (XATTN_HEAD_DIM ** -0.5)).astype(BF16)
            for i in range(heads_per_half):
                sl = slice(g * half + i * XATTN_HEAD_DIM, g * half + (i + 1) * XATTN_HEAD_DIM)
                out.append(_dot(q[:, i * XATTN_HEAD_DIM:(i + 1) * XATTN_HEAD_DIM], kt_ref[0, sl, :]))
        return out

    def attend(a, scs):
        y = x_ref[0, a:a + sub, :]
        for g in range(2):
            outs = []
            for i in range(heads_per_half):
                sc = scs[g * heads_per_half + i]
                sl = slice(g * half + i * XATTN_HEAD_DIM, g * half + (i + 1) * XATTN_HEAD_DIM)
                e = jnp.exp(sc - jnp.max(sc, axis=-1, keepdims=True))
                probs = e * (1.0 / jnp.sum(e, axis=-1, keepdims=True))
                outs.append(_dot(probs.astype(BF16), v_ref[0, :, sl]))
            o = jnp.concatenate(outs, axis=1).astype(BF16)
            y = y + _dot(o, wo_ref[g * half:(g + 1) * half, :])
        o_ref[0, a:a + sub, :] = y

    starts = list(range(0, ts, sub))
    scs = scores(starts[0])
    for i, a in enumerate(starts):
        nxt = scores(starts[i + 1]) if i + 1 < len(starts) else None
        attend(a, scs)
        scs = nxt


def _xattn(x, nw, wq, kt, v, wo, *, ts):
    b, s, d = x.shape
    const = _resident
    return pl.pallas_call(
        functools.partial(_xattn_kernel, ts=ts, sub=min(ts, SUB_TILE)),
        out_shape=jax.ShapeDtypeStruct(x.shape, F32),
        grid=(b, s // ts),
        in_specs=[
            pl.BlockSpec((1, ts, d), lambda i, j: (i, j, 0)),
            const(nw.shape), const(wq.shape),
            pl.BlockSpec((1, d, N_MEM), lambda i, j: (i, 0, 0)),
            pl.BlockSpec((1, N_MEM, d), lambda i, j: (i, 0, 0)),
            const(wo.shape),
        ],
        out_specs=pl.BlockSpec((1, ts, d), lambda i, j: (i, j, 0)),
        compiler_params=pltpu.CompilerParams(
            dimension_semantics=("parallel", "parallel"), vmem_limit_bytes=VMEM_LIMIT_BYTES),
        name="xattn",
    )(x, nw, wq, kt, v, wo)


def _ffn_kernel(x_ref, nw_ref, wup_ref, cw_ref, cb_ref, wdn_ref, nfw_ref, o_ref, ucarry, ubuf, act_s, *, ts):
    s = pl.program_id(1)

    @pl.when(s == 0)
    def _():
        ucarry[...] = jnp.zeros_like(ucarry)

    x = x_ref[0]
    h = _rms(x, nw_ref[...]).astype(BF16)

    def conv_cols(col0, slot):
        u = _dot(h, wup_ref[:, col0:col0 + FF_CHUNK])
        outs = []
        for k in range(FF_CHUNK // LANES):
            cols = slice(col0 + k * LANES, col0 + (k + 1) * LANES)
            uk = u[:, k * LANES:(k + 1) * LANES]
            buf = ubuf.at[slot, k]
            buf[:SUBLANES, :] = ucarry[:, cols]
            buf[SUBLANES:, :] = uk
            ucarry[:, cols] = uk[ts - SUBLANES:, :]
            cw = cw_ref[:, cols]
            outs.append(cb_ref[:, cols] + cw[0:1, :] * buf[SUBLANES - 2:SUBLANES - 2 + ts, :]
                        + cw[1:2, :] * buf[SUBLANES - 1:SUBLANES - 1 + ts, :] + cw[2:3, :] * uk)
        return jnp.concatenate(outs, axis=1)

    for j in range(D_FF // FF_CHUNK):
        gate = conv_cols(j * FF_CHUNK, (2 * j) % FF_UBUFS)
        val = conv_cols(D_FF + j * FF_CHUNK, (2 * j + 1) % FF_UBUFS)
        act_s[:, j * FF_CHUNK:(j + 1) * FF_CHUNK] = (gate * _sigmoid(gate) * val).astype(BF16)
    o_ref[0] = _rms(x + _dot(act_s[...], wdn_ref[...]), nfw_ref[...])


def _ffn(x, nw, wup, cw, cb, wdn, nfw, *, ts):
    b, s, d = x.shape
    const = _resident
    return pl.pallas_call(
        functools.partial(_ffn_kernel, ts=ts),
        out_shape=jax.ShapeDtypeStruct(x.shape, F32),
        grid=(b, s // ts),
        in_specs=[
            pl.BlockSpec((1, ts, d), lambda i, j: (i, j, 0)),
            const(nw.shape), const(wup.shape), const(cw.shape), const(cb.shape), const(wdn.shape),
            const(nfw.shape),
        ],
        out_specs=pl.BlockSpec((1, ts, d), lambda i, j: (i, j, 0)),
        scratch_shapes=[
            pltpu.VMEM((SUBLANES, 2 * D_FF), F32),
            pltpu.VMEM((FF_UBUFS, FF_CHUNK // LANES, SUBLANES + ts, LANES), F32),
            pltpu.VMEM((ts, D_FF), BF16),
        ],
        compiler_params=pltpu.CompilerParams(
            dimension_semantics=("parallel", "arbitrary"), vmem_limit_bytes=VMEM_LIMIT_BYTES),
        name="ffn",
    )(x, nw, wup, cw, cb, wdn, nfw)


def kernel(x, mem, norm_mix_w, w_in, pool_w, pool_scale, gk_w2, gk_b, gla_norm_w, w_out, norm_xattn_w,
           norm_mem_w, xattn_wq, xattn_wkv, xattn_wo, norm_ffn_w, ffn_w_up, ffn_conv_w, ffn_conv_b,
           ffn_w_down, norm_final_w):
    depth = w_in.shape[0]
    off_q = POOL_WIDTH
    off_v = off_q + 2 * GLA_KEY_WIDTH
    off_g = off_v + GLA_VALUE_WIDTH
    off_r = off_g + GLA_GATE_RANK
    row2 = lambda a: a.reshape(1, -1)
    for l in range(depth):
        wi = w_in[l]
        wg = jnp.pad(wi[:, off_g:off_r], ((0, 0), (0, LANES - GLA_GATE_RANK))).astype(BF16)
        gkw2 = jnp.pad(gk_w2[l], ((0, LANES - GLA_GATE_RANK), (0, 0))).astype(BF16)
        x = _mixer(
            x, row2(norm_mix_w[l]), wi[:, :off_q].astype(BF16), wi[:, off_q:off_v].astype(BF16),
            wi[:, off_v:off_g].astype(BF16), wg, wi[:, off_r:].astype(BF16), pool_w[l].astype(BF16),
            row2(pool_scale[l]), gkw2, row2(gk_b[l]), row2(gla_norm_w[l]), w_out[l].astype(BF16), ts=SEQ_TILE)
        wkv = xattn_wkv[l]
        kt, v = _mem_kv(mem, row2(norm_mem_w[l]), wkv[:, :D_MODEL].astype(BF16), wkv[:, D_MODEL:].astype(BF16))
        x = _xattn(x, row2(norm_xattn_w[l]), xattn_wq[l].astype(BF16), kt, v, xattn_wo[l].astype(BF16), ts=SEQ_TILE)
        assert depth == 1
        x = _ffn(x, row2(norm_ffn_w[l]), ffn_w_up[l].astype(BF16), ffn_conv_w[l], row2(ffn_conv_b[l]),
                 ffn_w_down[l].astype(BF16), row2(norm_final_w), ts=SEQ_TILE)
    return x
```

```python
import functools

import jax
import jax.numpy as jnp
from jax import lax
from jax.experimental import pallas as pl
from jax.experimental.pallas import tpu as pltpu

D_MODEL = 1024
N_MEM = 256
POOL_WIDTH = 512
POOL_GROUPS = 4
POOL_GROUP_DIM = 128
POOL_WINDOWS = (2, 4, 8, 16)
POOL_HALO = 16
GLA_VALUE_WIDTH = 512
GLA_KEY_WIDTH = 256
GLA_HEADS = 4
GLA_DK = 64
GLA_DV = 128
GLA_GATE_RANK = 16
CUM_PAD = 32
GLA_GATE_NORMALIZER = 16.0
GLA_CHUNK = 64
XATTN_HEADS = 4
XATTN_HEAD_DIM = 256
D_FF = 2816
CONV_WIDTH = 3
EPS = 1e-6

LANES = 128
SUBLANES = 8
FF_CHUNK = 256
FF_UBUFS = 4
SEQ_TILE = 1024
XATTN_TILE = 2048
SUB_TILE = 512
MEM_BATCH = 4
VMEM_LIMIT_BYTES = 56 * 1024 * 1024

BF16 = jnp.bfloat16
F32 = jnp.float32

_NT = (((1,), (1,)), ((), ()))
_TN = (((0,), (0,)), ((), ()))


def _dot(a, b):
    return jnp.dot(a, b, preferred_element_type=F32)


def _rms(x, w):
    return x * lax.rsqrt(jnp.mean(x * x, axis=-1, keepdims=True) + EPS) * w


def _sigmoid(x):
    return 1.0 / (1.0 + jnp.exp(-x))


def _resident(shape):
    return pl.BlockSpec(shape, lambda *_: (0,) * len(shape), pipeline_mode=pl.Buffered(1))


def _mixer_kernel(x_ref, nw_ref, wp_ref, wqk_ref, wv_ref, wg_ref, wr_ref, poolw_ref, pscale_ref,
                  gkw2_ref, gkb_ref, gnw_ref, wout_ref, o_ref,
                  pbuf, gbuf, state, q_s, k_s, v_s, rs_s, cat_s, *, ts, sub):
    s = pl.program_id(1)

    @pl.when(s == 0)
    def _():
        pbuf[:, :POOL_HALO, :] = jnp.zeros((POOL_GROUPS, POOL_HALO, LANES), F32)
        state[...] = jnp.zeros_like(state)

    row = lax.broadcasted_iota(jnp.int32, (sub, LANES), 0)
    crow = row % GLA_CHUNK
    lane = lax.broadcasted_iota(jnp.int32, (1, LANES), 1)
    first_head = lane < GLA_DK
    ci = lax.broadcasted_iota(jnp.int32, (GLA_CHUNK, LANES), 0)
    cj = lax.broadcasted_iota(jnp.int32, (GLA_CHUNK, LANES), 1) % GLA_CHUNK
    causal2 = ci >= cj
    pairs = range(GLA_HEADS // 2)

    def by_head(a):
        return jnp.concatenate([jnp.where(first_head, a, 0.0), jnp.where(first_head, 0.0, a)], axis=0)

    def project(a):
        rows = slice(a, a + sub)
        h = _rms(x_ref[0, rows, :], nw_ref[...]).astype(BF16)
        g_low = _dot(h, wg_ref[...])
        p = _dot(h, wp_ref[...])
        gate = _dot(g_low.astype(BF16), gkw2_ref[...]) + gkb_ref[...]
        lg = (jnp.minimum(gate, 0.0) - jnp.log1p(jnp.exp(-jnp.abs(gate)))) * (1.0 / GLA_GATE_NORMALIZER)
        g0 = CUM_PAD + a
        for k in pairs:
            gb = gbuf.at[k]
            if a == 0:
                gb[:CUM_PAD, :] = jnp.zeros((CUM_PAD, LANES), F32)
            gb[g0:g0 + sub, :] = lg[:, k * LANES:(k + 1) * LANES]
        shift = 1
        while shift < GLA_CHUNK:
            for k in pairs:
                gb = gbuf.at[k]
                cur = gb[g0:g0 + sub, :]
                prev = gb[g0 - shift:g0 - shift + sub, :]
                gb[g0:g0 + sub, :] = cur + jnp.where(crow >= shift, prev, 0.0)
            shift *= 2

        qk = _dot(h, wqk_ref[...])
        q_s[rows, :] = qk[:, :GLA_KEY_WIDTH]
        k_s[rows, :] = qk[:, GLA_KEY_WIDTH:]
        v_s[rows, :] = _dot(h, wv_ref[...]).astype(BF16)
        r = _dot(h, wr_ref[...])
        rs_s[rows, :] = r * _sigmoid(r)

        t1 = s * ts + a + row + 1
        p0 = POOL_HALO + a
        for g, w in enumerate(POOL_WINDOWS):
            lo, hi = g * POOL_GROUP_DIM, (g + 1) * POOL_GROUP_DIM
            pg = p[:, lo:hi]
            buf = pbuf.at[g]
            buf[p0:p0 + sub, :] = pg
            win = pg
            for d in range(1, w):
                win = win + buf[p0 - d:p0 - d + sub, :]
            if a + sub == ts:
                buf[:POOL_HALO, :] = pg[sub - POOL_HALO:, :]
            cnt = jnp.minimum(t1, w).astype(F32)
            pooled = win / cnt - pg
            mixed = _dot(pooled.astype(BF16), poolw_ref[g])
            cat_s[rows, lo:hi] = (mixed * pscale_ref[:, lo:hi]).astype(BF16)

    def gla_products(a):
        out = {}
        for c in range(sub // GLA_CHUNK):
            r0 = a + c * GLA_CHUNK
            rows = slice(r0, r0 + GLA_CHUNK)
            for j in pairs:
                kl = slice(j * LANES, (j + 1) * LANES)
                Gc = gbuf[j, CUM_PAD + r0:CUM_PAD + r0 + GLA_CHUNK, :]
                qc = q_s[rows, kl]
                kc = k_s[rows, kl]
                g_last = Gc[GLA_CHUNK - 1:GLA_CHUNK, :]
                q_dec = qc * (GLA_DK ** -0.5) * jnp.exp(Gc)
                k_dec = by_head(kc * jnp.exp(-Gc)).astype(BF16)
                k_end = by_head(kc * jnp.exp(g_last - Gc)).astype(BF16)
                sc = lax.dot_general(q_dec.astype(BF16), k_dec, _NT, preferred_element_type=F32)
                v_rows = jnp.concatenate(
                    [v_s[rows, (2 * j + hh) * GLA_DV:(2 * j + hh + 1) * GLA_DV] for hh in range(2)], axis=0)
                out[j, c] = dict(
                    decay=jnp.exp(g_last),
                    q_rows=by_head(q_dec).astype(BF16),
                    sc_rows=by_head(jnp.where(causal2, sc, 0.0)).astype(BF16),
                    v_rows=v_rows,
                    kv=lax.dot_general(v_rows, k_end, _TN, preferred_element_type=F32))
        return out

    def gla_states(prod, st):
        st = list(st)
        for c in range(sub // GLA_CHUNK):
            for j in pairs:
                prod[j, c]["state"] = st[j]
                st[j] = st[j] * prod[j, c]["decay"] + prod[j, c]["kv"]
        return st

    def gla_outputs(a, prod):
        for c in range(sub // GLA_CHUNK):
            r0 = a + c * GLA_CHUNK
            rows = slice(r0, r0 + GLA_CHUNK)
            for j in pairs:
                pc = prod[j, c]
                lhs = jnp.concatenate([pc["sc_rows"], pc["q_rows"]], axis=1)
                rhs = jnp.concatenate([pc["v_rows"], pc["state"].T.astype(BF16)], axis=0)
                o2 = _dot(lhs, rhs)
                for hh in range(2):
                    hd = 2 * j + hh
                    vl = slice(hd * GLA_DV, (hd + 1) * GLA_DV)
                    o = o2[hh * GLA_CHUNK:(hh + 1) * GLA_CHUNK, :]
                    gl = _rms(o, gnw_ref[...]) * rs_s[rows, vl]
                    cat_s[rows, POOL_WIDTH + hd * GLA_DV:POOL_WIDTH + (hd + 1) * GLA_DV] = gl.astype(BF16)

    def output(a):
        rows = slice(a, a + sub)
        o_ref[0, rows, :] = x_ref[0, rows, :] + _dot(cat_s[rows, :], wout_ref[...])

    starts = list(range(0, ts, sub))
    st = [state[j] for j in pairs]
    project(starts[0])
    if len(starts) > 1:
        project(starts[1])
    prod = gla_products(starts[0])
    st = gla_states(prod, st)
    for i, a in enumerate(starts):
        gla_outputs(a, prod)
        if i + 1 < len(starts):
            if i + 2 < len(starts):
                project(starts[i + 2])
            prod = gla_products(starts[i + 1])
            st = gla_states(prod, st)
        output(a)
    for j in pairs:
        state[j] = st[j]


def _mixer(x, nw, wp, wqk, wv, wg, wr, poolw, pscale, gkw2, gkb, gnw, wout, *, ts):
    b, s, d = x.shape
    const = _resident
    return pl.pallas_call(
        functools.partial(_mixer_kernel, ts=ts, sub=min(ts, SUB_TILE)),
        out_shape=jax.ShapeDtypeStruct(x.shape, F32),
        grid=(b, s // ts),
        in_specs=[
            pl.BlockSpec((1, ts, d), lambda i, j: (i, j, 0)),
            const(nw.shape), const(wp.shape), const(wqk.shape), const(wv.shape), const(wg.shape),
            const(wr.shape), const(poolw.shape), const(pscale.shape), const(gkw2.shape), const(gkb.shape),
            const(gnw.shape), const(wout.shape),
        ],
        out_specs=pl.BlockSpec((1, ts, d), lambda i, j: (i, j, 0)),
        scratch_shapes=[
            pltpu.VMEM((POOL_GROUPS, POOL_HALO + ts, LANES), F32),
            pltpu.VMEM((GLA_HEADS // 2, CUM_PAD + ts, LANES), F32),
            pltpu.VMEM((GLA_HEADS // 2, GLA_DV, LANES), F32),
            pltpu.VMEM((ts, GLA_KEY_WIDTH), F32),
            pltpu.VMEM((ts, GLA_KEY_WIDTH), F32),
            pltpu.VMEM((ts, GLA_VALUE_WIDTH), BF16),
            pltpu.VMEM((ts, GLA_VALUE_WIDTH), F32),
            pltpu.VMEM((ts, D_MODEL), BF16),
        ],
        compiler_params=pltpu.CompilerParams(
            dimension_semantics=("parallel", "arbitrary"), vmem_limit_bytes=VMEM_LIMIT_BYTES),
        name="mixer",
    )(x, nw, wp, wqk, wv, wg, wr, poolw, pscale, gkw2, gkb, gnw, wout)


def _mem_kv_kernel(mem_ref, nw_ref, wk_ref, wv_ref, kt_ref, v_ref):
    nb, m, d = mem_ref.shape
    h = _rms(mem_ref[...].reshape(nb * m, d), nw_ref[...]).astype(BF16)
    k = _dot(h, wk_ref[...])
    v_ref[...] = _dot(h, wv_ref[...]).astype(BF16).reshape(nb, m, d)
    for i in range(nb):
        kt_ref[i] = k[i * m:(i + 1) * m, :].T.astype(BF16)


def _mem_kv(mem, nw, wk, wv):
    b, m, d = mem.shape
    const = _resident
    return pl.pallas_call(
        _mem_kv_kernel,
        out_shape=(jax.ShapeDtypeStruct((b, d, m), BF16), jax.ShapeDtypeStruct((b, m, d), BF16)),
        grid=(b // MEM_BATCH,),
        in_specs=[pl.BlockSpec((MEM_BATCH, m, d), lambda i: (i, 0, 0)), const(nw.shape), const(wk.shape), const(wv.shape)],
        out_specs=(pl.BlockSpec((MEM_BATCH, d, m), lambda i: (i, 0, 0)), pl.BlockSpec((MEM_BATCH, m, d), lambda i: (i, 0, 0))),
        compiler_params=pltpu.CompilerParams(
            dimension_semantics=("parallel",), vmem_limit_bytes=VMEM_LIMIT_BYTES),
        name="mem_kv",
    )(mem, nw, wk, wv)


def _xattn_kernel(x_ref, nw_ref, wq_ref, kt_ref, v_ref, wo_ref, o_ref, *, ts, sub):
    half = D_MODEL // 2
    heads_per_half = XATTN_HEADS // 2

    def scores(a):
        h = _rms(x_ref[0, a:a + sub, :], nw_ref[...]).astype(BF16)
        out = []
        for g in range(2):
            q = (_dot(h, wq_ref[:, g * half:(g + 1) * half]) * (XATTN_HEAD_DIM ** -0.5)).astype(BF16)
            for i in range(heads_per_half):
                sl = slice(g * half + i * XATTN_HEAD_DIM, g * half + (i + 1) * XATTN_HEAD_DIM)
                out.append(_dot(q[:, i * XATTN_HEAD_DIM:(i + 1) * XATTN_HEAD_DIM], kt_ref[0, sl, :]))
        return out

    def attend(a, scs):
        y = x_ref[0, a:a + sub, :]
        for g in range(2):
            outs = []
            for i in range(heads_per_half):
                sc = scs[g * heads_per_half + i]
                sl = slice(g * half + i * XATTN_HEAD_DIM, g * half + (i + 1) * XATTN_HEAD_DIM)
                e = jnp.exp(sc - jnp.max(sc, axis=-1, keepdims=True))
                probs = e * (1.0 / jnp.sum(e, axis=-1, keepdims=True))
                outs.append(_dot(probs.astype(BF16), v_ref[0, :, sl]))
            o = jnp.concatenate(outs, axis=1).astype(BF16)
            y = y + _dot(o, wo_ref[g * half:(g + 1) * half, :])
        o_ref[0, a:a + sub, :] = y

    starts = list(range(0, ts, sub))
    scs = scores(starts[0])
    for i, a in enumerate(starts):
        nxt = scores(starts[i + 1]) if i + 1 < len(starts) else None
        attend(a, scs)
        scs = nxt


def _xattn(x, nw, wq, kt, v, wo, *, ts):
    b, s, d = x.shape
    const = _resident
    return pl.pallas_call(
        functools.partial(_xattn_kernel, ts=ts, sub=min(ts, SUB_TILE)),
        out_shape=jax.ShapeDtypeStruct(x.shape, F32),
        grid=(b, s // ts),
        in_specs=[
            pl.BlockSpec((1, ts, d), lambda i, j: (i, j, 0)),
            const(nw.shape), const(wq.shape),
            pl.BlockSpec((1, d, N_MEM), lambda i, j: (i, 0, 0)),
            pl.BlockSpec((1, N_MEM, d), lambda i, j: (i, 0, 0)),
            const(wo.shape),
        ],
        out_specs=pl.BlockSpec((1, ts, d), lambda i, j: (i, j, 0)),
        compiler_params=pltpu.CompilerParams(
            dimension_semantics=("parallel", "parallel"), vmem_limit_bytes=VMEM_LIMIT_BYTES),
        name="xattn",
    )(x, nw, wq, kt, v, wo)


def _ffn_kernel(x_ref, nw_ref, wup_ref, cw_ref, cb_ref, wdn_ref, nfw_ref, o_ref, ucarry, ubuf, act_s, *, ts):
    s = pl.program_id(1)

    @pl.when(s == 0)
    def _():
        ucarry[...] = jnp.zeros_like(ucarry)

    x = x_ref[0]
    h = _rms(x, nw_ref[...]).astype(BF16)

    def conv_cols(col0, slot):
        u = _dot(h, wup_ref[:, col0:col0 + FF_CHUNK])
        outs = []
        for k in range(FF_CHUNK // LANES):
            cols = slice(col0 + k * LANES, col0 + (k + 1) * LANES)
            uk = u[:, k * LANES:(k + 1) * LANES]
            buf = ubuf.at[slot, k]
            buf[:SUBLANES, :] = ucarry[:, cols]
            buf[SUBLANES:, :] = uk
            ucarry[:, cols] = uk[ts - SUBLANES:, :]
            cw = cw_ref[:, cols]
            outs.append(cb_ref[:, cols] + cw[0:1, :] * buf[SUBLANES - 2:SUBLANES - 2 + ts, :]
                        + cw[1:2, :] * buf[SUBLANES - 1:SUBLANES - 1 + ts, :] + cw[2:3, :] * uk)
        return jnp.concatenate(outs, axis=1)

    for j in range(D_FF // FF_CHUNK):
        gate = conv_cols(j * FF_CHUNK, (2 * j) % FF_UBUFS)
        val = conv_cols(D_FF + j * FF_CHUNK, (2 * j + 1) % FF_UBUFS)
        act_s[:, j * FF_CHUNK:(j + 1) * FF_CHUNK] = (gate * _sigmoid(gate) * val).astype(BF16)
    o_ref[0] = _rms(x + _dot(act_s[...], wdn_ref[...]), nfw_ref[...])


def _ffn(x, nw, wup, cw, cb, wdn, nfw, *, ts):
    b, s, d = x.shape
    const = _resident
    return pl.pallas_call(
        functools.partial(_ffn_kernel, ts=ts),
        out_shape=jax.ShapeDtypeStruct(x.shape, F32),
        grid=(b, s // ts),
        in_specs=[
            pl.BlockSpec((1, ts, d), lambda i, j: (i, j, 0)),
            const(nw.shape), const(wup.shape), const(cw.shape), const(cb.shape), const(wdn.shape),
            const(nfw.shape),
        ],
        out_specs=pl.BlockSpec((1, ts, d), lambda i, j: (i, j, 0)),
        scratch_shapes=[
            pltpu.VMEM((SUBLANES, 2 * D_FF), F32),
            pltpu.VMEM((FF_UBUFS, FF_CHUNK // LANES, SUBLANES + ts, LANES), F32),
            pltpu.VMEM((ts, D_FF), BF16),
        ],
        compiler_params=pltpu.CompilerParams(
            dimension_semantics=("parallel", "arbitrary"), vmem_limit_bytes=VMEM_LIMIT_BYTES),
        name="ffn",
    )(x, nw, wup, cw, cb, wdn, nfw)


def kernel(x, mem, norm_mix_w, w_in, pool_w, pool_scale, gk_w2, gk_b, gla_norm_w, w_out, norm_xattn_w,
           norm_mem_w, xattn_wq, xattn_wkv, xattn_wo, norm_ffn_w, ffn_w_up, ffn_conv_w, ffn_conv_b,
           ffn_w_down, norm_final_w):
    depth = w_in.shape[0]
    off_q = POOL_WIDTH
    off_v = off_q + 2 * GLA_KEY_WIDTH
    off_g = off_v + GLA_VALUE_WIDTH
    off_r = off_g + GLA_GATE_RANK
    row2 = lambda a: a.reshape(1, -1)
    for l in range(depth):
        wi = w_in[l]
        wg = jnp.pad(wi[:, off_g:off_r], ((0, 0), (0, LANES - GLA_GATE_RANK))).astype(BF16)
        gkw2 = jnp.pad(gk_w2[l], ((0, LANES - GLA_GATE_RANK), (0, 0))).astype(BF16)
        x = _mixer(
            x, row2(norm_mix_w[l]), wi[:, :off_q].astype(BF16), wi[:, off_q:off_v].astype(BF16),
            wi[:, off_v:off_g].astype(BF16), wg, wi[:, off_r:].astype(BF16), pool_w[l].astype(BF16),
            row2(pool_scale[l]), gkw2, row2(gk_b[l]), row2(gla_norm_w[l]), w_out[l].astype(BF16), ts=SEQ_TILE)
        wkv = xattn_wkv[l]
        kt, v = _mem_kv(mem, row2(norm_mem_w[l]), wkv[:, :D_MODEL].astype(BF16), wkv[:, D_MODEL:].astype(BF16))
        x = _xattn(x, row2(norm_xattn_w[l]), xattn_wq[l].astype(BF16), kt, v, xattn_wo[l].astype(BF16), ts=XATTN_TILE)
        assert depth == 1
        x = _ffn(x, row2(norm_ffn_w[l]), ffn_w_up[l].astype(BF16), ffn_conv_w[l], row2(ffn_conv_b[l]),
                 ffn_w_down[l].astype(BF16), row2(norm_final_w), ts=SEQ_TILE)
    return x
```

```python
import functools

import jax
import jax.numpy as jnp
from jax import lax
from jax.experimental import pallas as pl
from jax.experimental.pallas import tpu as pltpu

D_MODEL = 1024
N_MEM = 256
POOL_WIDTH = 512
POOL_GROUPS = 4
POOL_GROUP_DIM = 128
POOL_WINDOWS = (2, 4, 8, 16)
POOL_HALO = 16
GLA_VALUE_WIDTH = 512
GLA_KEY_WIDTH = 256
GLA_HEADS = 4
GLA_DK = 64
GLA_DV = 128
GLA_GATE_RANK = 16
CUM_PAD = 32
GLA_GATE_NORMALIZER = 16.0
GLA_CHUNK = 64
XATTN_HEADS = 4
XATTN_HEAD_DIM = 256
D_FF = 2816
CONV_WIDTH = 3
EPS = 1e-6

LANES = 128
SUBLANES = 8
FF_CHUNK = 256
FF_UBUFS = 4
SEQ_TILE = 1024
XATTN_TILE = 2048
SUB_TILE = 512
MIXER_SUB_TILE = 1024
MEM_BATCH = 4
VMEM_LIMIT_BYTES = 56 * 1024 * 1024

BF16 = jnp.bfloat16
F32 = jnp.float32

_NT = (((1,), (1,)), ((), ()))
_TN = (((0,), (0,)), ((), ()))


def _dot(a, b):
    return jnp.dot(a, b, preferred_element_type=F32)


def _rms(x, w):
    return x * lax.rsqrt(jnp.mean(x * x, axis=-1, keepdims=True) + EPS) * w


def _sigmoid(x):
    return 1.0 / (1.0 + jnp.exp(-x))


def _resident(shape):
    return pl.BlockSpec(shape, lambda *_: (0,) * len(shape), pipeline_mode=pl.Buffered(1))


def _mixer_kernel(x_ref, nw_ref, wp_ref, wqk_ref, wv_ref, wg_ref, wr_ref, poolw_ref, pscale_ref,
                  gkw2_ref, gkb_ref, gnw_ref, wout_ref, o_ref,
                  pbuf, gbuf, state, q_s, k_s, v_s, rs_s, cat_s, *, ts, sub):
    s = pl.program_id(1)

    @pl.when(s == 0)
    def _():
        pbuf[:, :POOL_HALO, :] = jnp.zeros((POOL_GROUPS, POOL_HALO, LANES), F32)
        state[...] = jnp.zeros_like(state)

    row = lax.broadcasted_iota(jnp.int32, (sub, LANES), 0)
    crow = row % GLA_CHUNK
    lane = lax.broadcasted_iota(jnp.int32, (1, LANES), 1)
    first_head = lane < GLA_DK
    ci = lax.broadcasted_iota(jnp.int32, (GLA_CHUNK, LANES), 0)
    cj = lax.broadcasted_iota(jnp.int32, (GLA_CHUNK, LANES), 1) % GLA_CHUNK
    causal2 = ci >= cj
    pairs = range(GLA_HEADS // 2)

    def by_head(a):
        return jnp.concatenate([jnp.where(first_head, a, 0.0), jnp.where(first_head, 0.0, a)], axis=0)

    def project(a):
        rows = slice(a, a + sub)
        h = _rms(x_ref[0, rows, :], nw_ref[...]).astype(BF16)
        g_low = _dot(h, wg_ref[...])
        p = _dot(h, wp_ref[...])
        gate = _dot(g_low.astype(BF16), gkw2_ref[...]) + gkb_ref[...]
        lg = (jnp.minimum(gate, 0.0) - jnp.log1p(jnp.exp(-jnp.abs(gate)))) * (1.0 / GLA_GATE_NORMALIZER)
        g0 = CUM_PAD + a
        for k in pairs:
            gb = gbuf.at[k]
            if a == 0:
                gb[:CUM_PAD, :] = jnp.zeros((CUM_PAD, LANES), F32)
            gb[g0:g0 + sub, :] = lg[:, k * LANES:(k + 1) * LANES]
        shift = 1
        while shift < GLA_CHUNK:
            for k in pairs:
                gb = gbuf.at[k]
                cur = gb[g0:g0 + sub, :]
                prev = gb[g0 - shift:g0 - shift + sub, :]
                gb[g0:g0 + sub, :] = cur + jnp.where(crow >= shift, prev, 0.0)
            shift *= 2

        qk = _dot(h, wqk_ref[...])
        q_s[rows, :] = qk[:, :GLA_KEY_WIDTH]
        k_s[rows, :] = qk[:, GLA_KEY_WIDTH:]
        v_s[rows, :] = _dot(h, wv_ref[...]).astype(BF16)
        r = _dot(h, wr_ref[...])
        rs_s[rows, :] = r * _sigmoid(r)

        t1 = s * ts + a + row + 1
        p0 = POOL_HALO + a
        for g, w in enumerate(POOL_WINDOWS):
            lo, hi = g * POOL_GROUP_DIM, (g + 1) * POOL_GROUP_DIM
            pg = p[:, lo:hi]
            buf = pbuf.at[g]
            buf[p0:p0 + sub, :] = pg
            win = pg
            for d in range(1, w):
                win = win + buf[p0 - d:p0 - d + sub, :]
            if a + sub == ts:
                buf[:POOL_HALO, :] = pg[sub - POOL_HALO:, :]
            cnt = jnp.minimum(t1, w).astype(F32)
            pooled = win / cnt - pg
            mixed = _dot(pooled.astype(BF16), poolw_ref[g])
            cat_s[rows, lo:hi] = (mixed * pscale_ref[:, lo:hi]).astype(BF16)

    def gla_products(a):
        out = {}
        for c in range(sub // GLA_CHUNK):
            r0 = a + c * GLA_CHUNK
            rows = slice(r0, r0 + GLA_CHUNK)
            for j in pairs:
                kl = slice(j * LANES, (j + 1) * LANES)
                Gc = gbuf[j, CUM_PAD + r0:CUM_PAD + r0 + GLA_CHUNK, :]
                qc = q_s[rows, kl]
                kc = k_s[rows, kl]
                g_last = Gc[GLA_CHUNK - 1:GLA_CHUNK, :]
                q_dec = qc * (GLA_DK ** -0.5) * jnp.exp(Gc)
                k_dec = by_head(kc * jnp.exp(-Gc)).astype(BF16)
                k_end = by_head(kc * jnp.exp(g_last - Gc)).astype(BF16)
                sc = lax.dot_general(q_dec.astype(BF16), k_dec, _NT, preferred_element_type=F32)
                v_rows = jnp.concatenate(
                    [v_s[rows, (2 * j + hh) * GLA_DV:(2 * j + hh + 1) * GLA_DV] for hh in range(2)], axis=0)
                out[j, c] = dict(
                    decay=jnp.exp(g_last),
                    q_rows=by_head(q_dec).astype(BF16),
                    sc_rows=by_head(jnp.where(causal2, sc, 0.0)).astype(BF16),
                    v_rows=v_rows,
                    kv=lax.dot_general(v_rows, k_end, _TN, preferred_element_type=F32))
        return out

    def gla_states(prod, st):
        st = list(st)
        for c in range(sub // GLA_CHUNK):
            for j in pairs:
                prod[j, c]["state"] = st[j]
                st[j] = st[j] * prod[j, c]["decay"] + prod[j, c]["kv"]
        return st

    def gla_outputs(a, prod):
        for c in range(sub // GLA_CHUNK):
            r0 = a + c * GLA_CHUNK
            rows = slice(r0, r0 + GLA_CHUNK)
            for j in pairs:
                pc = prod[j, c]
                lhs = jnp.concatenate([pc["sc_rows"], pc["q_rows"]], axis=1)
                rhs = jnp.concatenate([pc["v_rows"], pc["state"].T.astype(BF16)], axis=0)
                o2 = _dot(lhs, rhs)
                for hh in range(2):
                    hd = 2 * j + hh
                    vl = slice(hd * GLA_DV, (hd + 1) * GLA_DV)
                    o = o2[hh * GLA_CHUNK:(hh + 1) * GLA_CHUNK, :]
                    gl = _rms(o, gnw_ref[...]) * rs_s[rows, vl]
                    cat_s[rows, POOL_WIDTH + hd * GLA_DV:POOL_WIDTH + (hd + 1) * GLA_DV] = gl.astype(BF16)

    def output(a):
        rows = slice(a, a + sub)
        o_ref[0, rows, :] = x_ref[0, rows, :] + _dot(cat_s[rows, :], wout_ref[...])

    starts = list(range(0, ts, sub))
    st = [state[j] for j in pairs]
    project(starts[0])
    if len(starts) > 1:
        project(starts[1])
    prod = gla_products(starts[0])
    st = gla_states(prod, st)
    for i, a in enumerate(starts):
        gla_outputs(a, prod)
        if i + 1 < len(starts):
            if i + 2 < len(starts):
                project(starts[i + 2])
            prod = gla_products(starts[i + 1])
            st = gla_states(prod, st)
        output(a)
    for j in pairs:
        state[j] = st[j]


def _mixer(x, nw, wp, wqk, wv, wg, wr, poolw, pscale, gkw2, gkb, gnw, wout, *, ts):
    b, s, d = x.shape
    const = _resident
    return pl.pallas_call(
        functools.partial(_mixer_kernel, ts=ts, sub=min(ts, MIXER_SUB_TILE)),
        out_shape=jax.ShapeDtypeStruct(x.shape, F32),
        grid=(b, s // ts),
        in_specs=[
            pl.BlockSpec((1, ts, d), lambda i, j: (i, j, 0)),
            const(nw.shape), const(wp.shape), const(wqk.shape), const(wv.shape), const(wg.shape),
            const(wr.shape), const(poolw.shape), const(pscale.shape), const(gkw2.shape), const(gkb.shape),
            const(gnw.shape), const(wout.shape),
        ],
        out_specs=pl.BlockSpec((1, ts, d), lambda i, j: (i, j, 0)),
        scratch_shapes=[
            pltpu.VMEM((POOL_GROUPS, POOL_HALO + ts, LANES), F32),
            pltpu.VMEM((GLA_HEADS // 2, CUM_PAD + ts, LANES), F32),
            pltpu.VMEM((GLA_HEADS // 2, GLA_DV, LANES), F32),
            pltpu.VMEM((ts, GLA_KEY_WIDTH), F32),
            pltpu.VMEM((ts, GLA_KEY_WIDTH), F32),
            pltpu.VMEM((ts, GLA_VALUE_WIDTH), BF16),
            pltpu.VMEM((ts, GLA_VALUE_WIDTH), F32),
            pltpu.VMEM((ts, D_MODEL), BF16),
        ],
        compiler_params=pltpu.CompilerParams(
            dimension_semantics=("parallel", "arbitrary"), vmem_limit_bytes=VMEM_LIMIT_BYTES),
        name="mixer",
    )(x, nw, wp, wqk, wv, wg, wr, poolw, pscale, gkw2, gkb, gnw, wout)


def _mem_kv_kernel(mem_ref, nw_ref, wk_ref, wv_ref, kt_ref, v_ref):
    nb, m, d = mem_ref.shape
    h = _rms(mem_ref[...].reshape(nb * m, d), nw_ref[...]).astype(BF16)
    k = _dot(h, wk_ref[...])
    v_ref[...] = _dot(h, wv_ref[...]).astype(BF16).reshape(nb, m, d)
    for i in range(nb):
        kt_ref[i] = k[i * m:(i + 1) * m, :].T.astype(BF16)


def _mem_kv(mem, nw, wk, wv):
    b, m, d = mem.shape
    const = _resident
    return pl.pallas_call(
        _mem_kv_kernel,
        out_shape=(jax.ShapeDtypeStruct((b, d, m), BF16), jax.ShapeDtypeStruct((b, m, d), BF16)),
        grid=(b // MEM_BATCH,),
        in_specs=[pl.BlockSpec((MEM_BATCH, m, d), lambda i: (i, 0, 0)), const(nw.shape), const(wk.shape), const(wv.shape)],
        out_specs=(pl.BlockSpec((MEM_BATCH, d, m), lambda i: (i, 0, 0)), pl.BlockSpec((MEM_BATCH, m, d), lambda i: (i, 0, 0))),
        compiler_params=pltpu.CompilerParams(
            dimension_semantics=("parallel",), vmem_limit_bytes=VMEM_LIMIT_BYTES),
        name="mem_kv",
    )(mem, nw, wk, wv)


def _xattn_kernel(x_ref, nw_ref, wq_ref, kt_ref, v_ref, wo_ref, o_ref, *, ts, sub):
    half = D_MODEL // 2
    heads_per_half = XATTN_HEADS // 2

    def scores(a):
        h = _rms(x_ref[0, a:a + sub, :], nw_ref[...]).astype(BF16)
        out = []
        for g in range(2):
            q = (_dot(h, wq_ref[:, g * half:(g + 1) * half]) * (XATTN_HEAD_DIM ** -0.5)).astype(BF16)
            for i in range(heads_per_half):
                sl = slice(g * half + i * XATTN_HEAD_DIM, g * half + (i + 1) * XATTN_HEAD_DIM)
                out.append(_dot(q[:, i * XATTN_HEAD_DIM:(i + 1) * XATTN_HEAD_DIM], kt_ref[0, sl, :]))
        return out

    def attend(a, scs):
        y = x_ref[0, a:a + sub, :]
        for g in range(2):
            outs = []
            for i in range(heads_per_half):
                sc = scs[g * heads_per_half + i]
                sl = slice(g * half + i * XATTN_HEAD_DIM, g * half + (i + 1) * XATTN_HEAD_DIM)
                e = jnp.exp(sc - jnp.max(sc, axis=-1, keepdims=True))
                probs = e * (1.0 / jnp.sum(e, axis=-1, keepdims=True))
                outs.append(_dot(probs.astype(BF16), v_ref[0, :, sl]))
            o = jnp.concatenate(outs, axis=1).astype(BF16)
            y = y + _dot(o, wo_ref[g * half:(g + 1) * half, :])
        o_ref[0, a:a + sub, :] = y

    starts = list(range(0, ts, sub))
    scs = scores(starts[0])
    for i, a in enumerate(starts):
        nxt = scores(starts[i + 1]) if i + 1 < len(starts) else None
        attend(a, scs)
        scs = nxt


def _xattn(x, nw, wq, kt, v, wo, *, ts):
    b, s, d = x.shape
    const = _resident
    return pl.pallas_call(
        functools.partial(_xattn_kernel, ts=ts, sub=min(ts, SUB_TILE)),
        out_shape=jax.ShapeDtypeStruct(x.shape, F32),
        grid=(b, s // ts),
        in_specs=[
            pl.BlockSpec((1, ts, d), lambda i, j: (i, j, 0)),
            const(nw.shape), const(wq.shape),
            pl.BlockSpec((1, d, N_MEM), lambda i, j: (i, 0, 0)),
            pl.BlockSpec((1, N_MEM, d), lambda i, j: (i, 0, 0)),
            const(wo.shape),
        ],
        out_specs=pl.BlockSpec((1, ts, d), lambda i, j: (i, j, 0)),
        compiler_params=pltpu.CompilerParams(
            dimension_semantics=("parallel", "parallel"), vmem_limit_bytes=VMEM_LIMIT_BYTES),
        name="xattn",
    )(x, nw, wq, kt, v, wo)


def _ffn_kernel(x_ref, nw_ref, wup_ref, cw_ref, cb_ref, wdn_ref, nfw_ref, o_ref, ucarry, ubuf, act_s, *, ts):
    s = pl.program_id(1)

    @pl.when(s == 0)
    def _():
        ucarry[...] = jnp.zeros_like(ucarry)

    x = x_ref[0]
    h = _rms(x, nw_ref[...]).astype(BF16)

    def conv_cols(col0, slot):
        u = _dot(h, wup_ref[:, col0:col0 + FF_CHUNK])
        outs = []
        for k in range(FF_CHUNK // LANES):
            cols = slice(col0 + k * LANES, col0 + (k + 1) * LANES)
            uk = u[:, k * LANES:(k + 1) * LANES]
            buf = ubuf.at[slot, k]
            buf[:SUBLANES, :] = ucarry[:, cols]
            buf[SUBLANES:, :] = uk
            ucarry[:, cols] = uk[ts - SUBLANES:, :]
            cw = cw_ref[:, cols]
            outs.append(cb_ref[:, cols] + cw[0:1, :] * buf[SUBLANES - 2:SUBLANES - 2 + ts, :]
                        + cw[1:2, :] * buf[SUBLANES - 1:SUBLANES - 1 + ts, :] + cw[2:3, :] * uk)
        return jnp.concatenate(outs, axis=1)

    for j in range(D_FF // FF_CHUNK):
        gate = conv_cols(j * FF_CHUNK, (2 * j) % FF_UBUFS)
        val = conv_cols(D_FF + j * FF_CHUNK, (2 * j + 1) % FF_UBUFS)
        act_s[:, j * FF_CHUNK:(j + 1) * FF_CHUNK] = (gate * _sigmoid(gate) * val).astype(BF16)
    o_ref[0] = _rms(x + _dot(act_s[...], wdn_ref[...]), nfw_ref[...])


def _ffn(x, nw, wup, cw, cb, wdn, nfw, *, ts):
    b, s, d = x.shape
    const = _resident
    return pl.pallas_call(
        functools.partial(_ffn_kernel, ts=ts),
        out_shape=jax.ShapeDtypeStruct(x.shape, F32),
        grid=(b, s // ts),
        in_specs=[
            pl.BlockSpec((1, ts, d), lambda i, j: (i, j, 0)),
            const(nw.shape), const(wup.shape), const(cw.shape), const(cb.shape), const(wdn.shape),
            const(nfw.shape),
        ],
        out_specs=pl.BlockSpec((1, ts, d), lambda i, j: (i, j, 0)),
        scratch_shapes=[
            pltpu.VMEM((SUBLANES, 2 * D_FF), F32),
            pltpu.VMEM((FF_UBUFS, FF_CHUNK // LANES, SUBLANES + ts, LANES), F32),
            pltpu.VMEM((ts, D_FF), BF16),
        ],
        compiler_params=pltpu.CompilerParams(
            dimension_semantics=("parallel", "arbitrary"), vmem_limit_bytes=VMEM_LIMIT_BYTES),
        name="ffn",
    )(x, nw, wup, cw, cb, wdn, nfw)


def kernel(x, mem, norm_mix_w, w_in, pool_w, pool_scale, gk_w2, gk_b, gla_norm_w, w_out, norm_xattn_w,
           norm_mem_w, xattn_wq, xattn_wkv, xattn_wo, norm_ffn_w, ffn_w_up, ffn_conv_w, ffn_conv_b,
           ffn_w_down, norm_final_w):
    depth = w_in.shape[0]
    off_q = POOL_WIDTH
    off_v = off_q + 2 * GLA_KEY_WIDTH
    off_g = off_v + GLA_VALUE_WIDTH
    off_r = off_g + GLA_GATE_RANK
    row2 = lambda a: a.reshape(1, -1)
    for l in range(depth):
        wi = w_in[l]
        wg = jnp.pad(wi[:, off_g:off_r], ((0, 0), (0, LANES - GLA_GATE_RANK))).astype(BF16)
        gkw2 = jnp.pad(gk_w2[l], ((0, LANES - GLA_GATE_RANK), (0, 0))).astype(BF16)
        x = _mixer(
            x, row2(norm_mix_w[l]), wi[:, :off_q].astype(BF16), wi[:, off_q:off_v].astype(BF16),
            wi[:, off_v:off_g].astype(BF16), wg, wi[:, off_r:].astype(BF16), pool_w[l].astype(BF16),
            row2(pool_scale[l]), gkw2, row2(gk_b[l]), row2(gla_norm_w[l]), w_out[l].astype(BF16), ts=SEQ_TILE)
        wkv = xattn_wkv[l]
        kt, v = _mem_kv(mem, row2(norm_mem_w[l]), wkv[:, :D_MODEL].astype(BF16), wkv[:, D_MODEL:].astype(BF16))
        x = _xattn(x, row2(norm_xattn_w[l]), xattn_wq[l].astype(BF16), kt, v, xattn_wo[l].astype(BF16), ts=XATTN_TILE)
        assert depth == 1
        x = _ffn(x, row2(norm_ffn_w[l]), ffn_w_up[l].astype(BF16), ffn_conv_w[l], row2(ffn_conv_b[l]),
                 ffn_w_down[l].astype(BF16), row2(norm_final_w), ts=SEQ_TILE)
    return x
```

```python
import functools

import jax
import jax.numpy as jnp
from jax import lax
from jax.experimental import pallas as pl
from jax.experimental.pallas import tpu as pltpu

D_MODEL = 1024
N_MEM = 256
POOL_WIDTH = 512
POOL_GROUPS = 4
POOL_GROUP_DIM = 128
POOL_WINDOWS = (2, 4, 8, 16)
POOL_HALO = 16
GLA_VALUE_WIDTH = 512
GLA_KEY_WIDTH = 256
GLA_HEADS = 4
GLA_DK = 64
GLA_DV = 128
GLA_GATE_RANK = 16
CUM_PAD = 32
GLA_GATE_NORMALIZER = 16.0
GLA_CHUNK = 64
XATTN_HEADS = 4
XATTN_HEAD_DIM = 256
D_FF = 2816
CONV_WIDTH = 3
EPS = 1e-6

LANES = 128
SUBLANES = 8
FF_CHUNK = 256
FF_UBUFS = 4
SEQ_TILE = 1024
SUB_TILE = 1024
MIXER_SUB_TILE = 1024
VMEM_LIMIT_BYTES = 56 * 1024 * 1024

BF16 = jnp.bfloat16
F32 = jnp.float32

_NT = (((1,), (1,)), ((), ()))
_TN = (((0,), (0,)), ((), ()))


def _dot(a, b):
    return jnp.dot(a, b, preferred_element_type=F32)


def _rms(x, w):
    return x * lax.rsqrt(jnp.mean(x * x, axis=-1, keepdims=True) + EPS) * w


def _sigmoid(x):
    return 1.0 / (1.0 + jnp.exp(-x))


def _resident(shape):
    return pl.BlockSpec(shape, lambda *_: (0,) * len(shape), pipeline_mode=pl.Buffered(1))


def _mixer_kernel(x_ref, nw_ref, wp_ref, wqk_ref, wv_ref, wg_ref, wr_ref, poolw_ref, pscale_ref,
                  gkw2_ref, gkb_ref, gnw_ref, wout_ref, o_ref,
                  pbuf, gbuf, state, q_s, k_s, v_s, rs_s, cat_s, *, ts, sub):
    s = pl.program_id(1)

    @pl.when(s == 0)
    def _():
        pbuf[:, :POOL_HALO, :] = jnp.zeros((POOL_GROUPS, POOL_HALO, LANES), F32)
        state[...] = jnp.zeros_like(state)

    row = lax.broadcasted_iota(jnp.int32, (sub, LANES), 0)
    crow = row % GLA_CHUNK
    lane = lax.broadcasted_iota(jnp.int32, (1, LANES), 1)
    first_head = lane < GLA_DK
    ci = lax.broadcasted_iota(jnp.int32, (GLA_CHUNK, LANES), 0)
    cj = lax.broadcasted_iota(jnp.int32, (GLA_CHUNK, LANES), 1) % GLA_CHUNK
    causal2 = ci >= cj
    pairs = range(GLA_HEADS // 2)

    def by_head(a):
        return jnp.concatenate([jnp.where(first_head, a, 0.0), jnp.where(first_head, 0.0, a)], axis=0)

    def project(a):
        rows = slice(a, a + sub)
        h = _rms(x_ref[0, rows, :], nw_ref[...]).astype(BF16)
        g_low = _dot(h, wg_ref[...])
        p = _dot(h, wp_ref[...])
        gate = _dot(g_low.astype(BF16), gkw2_ref[...]) + gkb_ref[...]
        lg = (jnp.minimum(gate, 0.0) - jnp.log1p(jnp.exp(-jnp.abs(gate)))) * (1.0 / GLA_GATE_NORMALIZER)
        g0 = CUM_PAD + a
        for k in pairs:
            gb = gbuf.at[k]
            if a == 0:
                gb[:CUM_PAD, :] = jnp.zeros((CUM_PAD, LANES), F32)
            gb[g0:g0 + sub, :] = lg[:, k * LANES:(k + 1) * LANES]
        shift = 1
        while shift < GLA_CHUNK:
            for k in pairs:
                gb = gbuf.at[k]
                cur = gb[g0:g0 + sub, :]
                prev = gb[g0 - shift:g0 - shift + sub, :]
                gb[g0:g0 + sub, :] = cur + jnp.where(crow >= shift, prev, 0.0)
            shift *= 2

        qk = _dot(h, wqk_ref[...])
        q_s[rows, :] = qk[:, :GLA_KEY_WIDTH]
        k_s[rows, :] = qk[:, GLA_KEY_WIDTH:]
        v_s[rows, :] = _dot(h, wv_ref[...]).astype(BF16)
        r = _dot(h, wr_ref[...])
        rs_s[rows, :] = r * _sigmoid(r)

        t1 = s * ts + a + row + 1
        p0 = POOL_HALO + a
        for g, w in enumerate(POOL_WINDOWS):
            lo, hi = g * POOL_GROUP_DIM, (g + 1) * POOL_GROUP_DIM
            pg = p[:, lo:hi]
            buf = pbuf.at[g]
            buf[p0:p0 + sub, :] = pg
            win = pg
            for d in range(1, w):
                win = win + buf[p0 - d:p0 - d + sub, :]
            if a + sub == ts:
                buf[:POOL_HALO, :] = pg[sub - POOL_HALO:, :]
            cnt = jnp.minimum(t1, w).astype(F32)
            pooled = win / cnt - pg
            mixed = _dot(pooled.astype(BF16), poolw_ref[g])
            cat_s[rows, lo:hi] = (mixed * pscale_ref[:, lo:hi]).astype(BF16)

    def gla_products(a):
        out = {}
        for c in range(sub // GLA_CHUNK):
            r0 = a + c * GLA_CHUNK
            rows = slice(r0, r0 + GLA_CHUNK)
            for j in pairs:
                kl = slice(j * LANES, (j + 1) * LANES)
                Gc = gbuf[j, CUM_PAD + r0:CUM_PAD + r0 + GLA_CHUNK, :]
                qc = q_s[rows, kl]
                kc = k_s[rows, kl]
                g_last = Gc[GLA_CHUNK - 1:GLA_CHUNK, :]
                q_dec = qc * (GLA_DK ** -0.5) * jnp.exp(Gc)
                k_dec = by_head(kc * jnp.exp(-Gc)).astype(BF16)
                k_end = by_head(kc * jnp.exp(g_last - Gc)).astype(BF16)
                sc = lax.dot_general(q_dec.astype(BF16), k_dec, _NT, preferred_element_type=F32)
                v_rows = jnp.concatenate(
                    [v_s[rows, (2 * j + hh) * GLA_DV:(2 * j + hh + 1) * GLA_DV] for hh in range(2)], axis=0)
                out[j, c] = dict(
                    decay=jnp.exp(g_last),
                    q_rows=by_head(q_dec).astype(BF16),
                    sc_rows=by_head(jnp.where(causal2, sc, 0.0)).astype(BF16),
                    v_rows=v_rows,
                    kv=lax.dot_general(v_rows, k_end, _TN, preferred_element_type=F32))
        return out

    def gla_states(prod, st):
        st = list(st)
        for c in range(sub // GLA_CHUNK):
            for j in pairs:
                prod[j, c]["state"] = st[j]
                st[j] = st[j] * prod[j, c]["decay"] + prod[j, c]["kv"]
        return st

    def gla_outputs(a, prod):
        for c in range(sub // GLA_CHUNK):
            r0 = a + c * GLA_CHUNK
            rows = slice(r0, r0 + GLA_CHUNK)
            for j in pairs:
                pc = prod[j, c]
                lhs = jnp.concatenate([pc["sc_rows"], pc["q_rows"]], axis=1)
                rhs = jnp.concatenate([pc["v_rows"], pc["state"].T.astype(BF16)], axis=0)
                o2 = _dot(lhs, rhs)
                for hh in range(2):
                    hd = 2 * j + hh
                    vl = slice(hd * GLA_DV, (hd + 1) * GLA_DV)
                    o = o2[hh * GLA_CHUNK:(hh + 1) * GLA_CHUNK, :]
                    gl = _rms(o, gnw_ref[...]) * rs_s[rows, vl]
                    cat_s[rows, POOL_WIDTH + hd * GLA_DV:POOL_WIDTH + (hd + 1) * GLA_DV] = gl.astype(BF16)

    def output(a):
        rows = slice(a, a + sub)
        o_ref[0, rows, :] = x_ref[0, rows, :] + _dot(cat_s[rows, :], wout_ref[...])

    starts = list(range(0, ts, sub))
    st = [state[j] for j in pairs]
    project(starts[0])
    if len(starts) > 1:
        project(starts[1])
    prod = gla_products(starts[0])
    st = gla_states(prod, st)
    for i, a in enumerate(starts):
        gla_outputs(a, prod)
        if i + 1 < len(starts):
            if i + 2 < len(starts):
                project(starts[i + 2])
            prod = gla_products(starts[i + 1])
            st = gla_states(prod, st)
        output(a)
    for j in pairs:
        state[j] = st[j]


def _mixer(x, nw, wp, wqk, wv, wg, wr, poolw, pscale, gkw2, gkb, gnw, wout, *, ts):
    b, s, d = x.shape
    const = _resident
    return pl.pallas_call(
        functools.partial(_mixer_kernel, ts=ts, sub=min(ts, MIXER_SUB_TILE)),
        out_shape=jax.ShapeDtypeStruct(x.shape, F32),
        grid=(b, s // ts),
        in_specs=[
            pl.BlockSpec((1, ts, d), lambda i, j: (i, j, 0)),
            const(nw.shape), const(wp.shape), const(wqk.shape), const(wv.shape), const(wg.shape),
            const(wr.shape), const(poolw.shape), const(pscale.shape), const(gkw2.shape), const(gkb.shape),
            const(gnw.shape), const(wout.shape),
        ],
        out_specs=pl.BlockSpec((1, ts, d), lambda i, j: (i, j, 0)),
        scratch_shapes=[
            pltpu.VMEM((POOL_GROUPS, POOL_HALO + ts, LANES), F32),
            pltpu.VMEM((GLA_HEADS // 2, CUM_PAD + ts, LANES), F32),
            pltpu.VMEM((GLA_HEADS // 2, GLA_DV, LANES), F32),
            pltpu.VMEM((ts, GLA_KEY_WIDTH), F32),
            pltpu.VMEM((ts, GLA_KEY_WIDTH), F32),
            pltpu.VMEM((ts, GLA_VALUE_WIDTH), BF16),
            pltpu.VMEM((ts, GLA_VALUE_WIDTH), F32),
            pltpu.VMEM((ts, D_MODEL), BF16),
        ],
        compiler_params=pltpu.CompilerParams(
            dimension_semantics=("parallel", "arbitrary"), vmem_limit_bytes=VMEM_LIMIT_BYTES),
        name="mixer",
    )(x, nw, wp, wqk, wv, wg, wr, poolw, pscale, gkw2, gkb, gnw, wout)


def _xattn_kernel(x_ref, nw_ref, wq_ref, mem_ref, nmw_ref, wkt_ref, wvt_ref, wo_ref, o_ref, kt_s, vt_s, *, ts, sub):
    half = D_MODEL // 2
    heads_per_half = XATTN_HEADS // 2

    m = _rms(mem_ref[0], nmw_ref[...]).astype(BF16)
    kt_s[...] = lax.dot_general(wkt_ref[...], m, _NT, preferred_element_type=F32).astype(BF16)
    vt_s[...] = lax.dot_general(wvt_ref[...], m, _NT, preferred_element_type=F32).astype(BF16)

    def scores(a):
        h = _rms(x_ref[0, a:a + sub, :], nw_ref[...]).astype(BF16)
        out = []
        for g in range(2):
            q = (_dot(h, wq_ref[:, g * half:(g + 1) * half]) * (XATTN_HEAD_DIM ** -0.5)).astype(BF16)
            for i in range(heads_per_half):
                sl = slice(g * half + i * XATTN_HEAD_DIM, g * half + (i + 1) * XATTN_HEAD_DIM)
                out.append(_dot(q[:, i * XATTN_HEAD_DIM:(i + 1) * XATTN_HEAD_DIM], kt_s[sl, :]))
        return out

    def attend(a, scs):
        y = x_ref[0, a:a + sub, :]
        for g in range(2):
            outs = []
            for i in range(heads_per_half):
                sc = scs[g * heads_per_half + i]
                sl = slice(g * half + i * XATTN_HEAD_DIM, g * half + (i + 1) * XATTN_HEAD_DIM)
                e = jnp.exp(sc - jnp.max(sc, axis=-1, keepdims=True))
                probs = e * (1.0 / jnp.sum(e, axis=-1, keepdims=True))
                outs.append(lax.dot_general(probs.astype(BF16), vt_s[sl, :], _NT, preferred_element_type=F32))
            o = jnp.concatenate(outs, axis=1).astype(BF16)
            y = y + _dot(o, wo_ref[g * half:(g + 1) * half, :])
        o_ref[0, a:a + sub, :] = y

    starts = list(range(0, ts, sub))
    scs = scores(starts[0])
    for i, a in enumerate(starts):
        nxt = scores(starts[i + 1]) if i + 1 < len(starts) else None
        attend(a, scs)
        scs = nxt


def _xattn(x, mem, nw, wq, nmw, wkt, wvt, wo):
    b, s, d = x.shape
    n_mem = mem.shape[1]
    const = _resident
    return pl.pallas_call(
        functools.partial(_xattn_kernel, ts=s, sub=min(s, SUB_TILE)),
        out_shape=jax.ShapeDtypeStruct(x.shape, F32),
        grid=(b,),
        in_specs=[
            pl.BlockSpec((1, s, d), lambda i: (i, 0, 0)),
            const(nw.shape), const(wq.shape),
            pl.BlockSpec((1, n_mem, d), lambda i: (i, 0, 0)),
            const(nmw.shape), const(wkt.shape), const(wvt.shape), const(wo.shape),
        ],
        out_specs=pl.BlockSpec((1, s, d), lambda i: (i, 0, 0)),
        scratch_shapes=[pltpu.VMEM((d, n_mem), BF16), pltpu.VMEM((d, n_mem), BF16)],
        compiler_params=pltpu.CompilerParams(
            dimension_semantics=("parallel",), vmem_limit_bytes=VMEM_LIMIT_BYTES),
        name="xattn",
    )(x, nw, wq, mem, nmw, wkt, wvt, wo)


def _ffn_kernel(x_ref, nw_ref, wup_ref, cw_ref, cb_ref, wdn_ref, nfw_ref, o_ref, ucarry, ubuf, act_s, *, ts):
    s = pl.program_id(1)

    @pl.when(s == 0)
    def _():
        ucarry[...] = jnp.zeros_like(ucarry)

    x = x_ref[0]
    h = _rms(x, nw_ref[...]).astype(BF16)

    def conv_cols(col0, slot):
        u = _dot(h, wup_ref[:, col0:col0 + FF_CHUNK])
        outs = []
        for k in range(FF_CHUNK // LANES):
            cols = slice(col0 + k * LANES, col0 + (k + 1) * LANES)
            uk = u[:, k * LANES:(k + 1) * LANES]
            buf = ubuf.at[slot, k]
            buf[:SUBLANES, :] = ucarry[:, cols]
            buf[SUBLANES:, :] = uk
            ucarry[:, cols] = uk[ts - SUBLANES:, :]
            cw = cw_ref[:, cols]
            outs.append(cb_ref[:, cols] + cw[0:1, :] * buf[SUBLANES - 2:SUBLANES - 2 + ts, :]
                        + cw[1:2, :] * buf[SUBLANES - 1:SUBLANES - 1 + ts, :] + cw[2:3, :] * uk)
        return jnp.concatenate(outs, axis=1)

    for j in range(D_FF // FF_CHUNK):
        gate = conv_cols(j * FF_CHUNK, (2 * j) % FF_UBUFS)
        val = conv_cols(D_FF + j * FF_CHUNK, (2 * j + 1) % FF_UBUFS)
        act_s[:, j * FF_CHUNK:(j + 1) * FF_CHUNK] = (gate * _sigmoid(gate) * val).astype(BF16)
    o_ref[0] = _rms(x + _dot(act_s[...], wdn_ref[...]), nfw_ref[...])


def _ffn(x, nw, wup, cw, cb, wdn, nfw, *, ts):
    b, s, d = x.shape
    const = _resident
    return pl.pallas_call(
        functools.partial(_ffn_kernel, ts=ts),
        out_shape=jax.ShapeDtypeStruct(x.shape, F32),
        grid=(b, s // ts),
        in_specs=[
            pl.BlockSpec((1, ts, d), lambda i, j: (i, j, 0)),
            const(nw.shape), const(wup.shape), const(cw.shape), const(cb.shape), const(wdn.shape),
            const(nfw.shape),
        ],
        out_specs=pl.BlockSpec((1, ts, d), lambda i, j: (i, j, 0)),
        scratch_shapes=[
            pltpu.VMEM((SUBLANES, 2 * D_FF), F32),
            pltpu.VMEM((FF_UBUFS, FF_CHUNK // LANES, SUBLANES + ts, LANES), F32),
            pltpu.VMEM((ts, D_FF), BF16),
        ],
        compiler_params=pltpu.CompilerParams(
            dimension_semantics=("parallel", "arbitrary"), vmem_limit_bytes=VMEM_LIMIT_BYTES),
        name="ffn",
    )(x, nw, wup, cw, cb, wdn, nfw)


def kernel(x, mem, norm_mix_w, w_in, pool_w, pool_scale, gk_w2, gk_b, gla_norm_w, w_out, norm_xattn_w,
           norm_mem_w, xattn_wq, xattn_wkv, xattn_wo, norm_ffn_w, ffn_w_up, ffn_conv_w, ffn_conv_b,
           ffn_w_down, norm_final_w):
    depth = w_in.shape[0]
    off_q = POOL_WIDTH
    off_v = off_q + 2 * GLA_KEY_WIDTH
    off_g = off_v + GLA_VALUE_WIDTH
    off_r = off_g + GLA_GATE_RANK
    row2 = lambda a: a.reshape(1, -1)
    for l in range(depth):
        wi = w_in[l]
        wg = jnp.pad(wi[:, off_g:off_r], ((0, 0), (0, LANES - GLA_GATE_RANK))).astype(BF16)
        gkw2 = jnp.pad(gk_w2[l], ((0, LANES - GLA_GATE_RANK), (0, 0))).astype(BF16)
        x = _mixer(
            x, row2(norm_mix_w[l]), wi[:, :off_q].astype(BF16), wi[:, off_q:off_v].astype(BF16),
            wi[:, off_v:off_g].astype(BF16), wg, wi[:, off_r:].astype(BF16), pool_w[l].astype(BF16),
            row2(pool_scale[l]), gkw2, row2(gk_b[l]), row2(gla_norm_w[l]), w_out[l].astype(BF16), ts=SEQ_TILE)
        wkv = xattn_wkv[l]
        x = _xattn(x, mem, row2(norm_xattn_w[l]), xattn_wq[l].astype(BF16), row2(norm_mem_w[l]),
                   wkv[:, :D_MODEL].T.astype(BF16), wkv[:, D_MODEL:].T.astype(BF16), xattn_wo[l].astype(BF16))
        assert depth == 1
        x = _ffn(x, row2(norm_ffn_w[l]), ffn_w_up[l].astype(BF16), ffn_conv_w[l], row2(ffn_conv_b[l]),
                 ffn_w_down[l].astype(BF16), row2(norm_final_w), ts=SEQ_TILE)
    return x
```

```python
import functools

import jax
import jax.numpy as jnp
from jax import lax
from jax.experimental import pallas as pl
from jax.experimental.pallas import tpu as pltpu

D_MODEL = 1024
N_MEM = 256
POOL_WIDTH = 512
POOL_GROUPS = 4
POOL_GROUP_DIM = 128
POOL_WINDOWS = (2, 4, 8, 16)
POOL_HALO = 16
GLA_VALUE_WIDTH = 512
GLA_KEY_WIDTH = 256
GLA_HEADS = 4
GLA_DK = 64
GLA_DV = 128
GLA_GATE_RANK = 16
CUM_PAD = 32
GLA_GATE_NORMALIZER = 16.0
GLA_CHUNK = 64
XATTN_HEADS = 4
XATTN_HEAD_DIM = 256
D_FF = 2816
CONV_WIDTH = 3
EPS = 1e-6

LANES = 128
SUBLANES = 8
FF_CHUNK = 256
FF_UBUFS = 2
SEQ_TILE = 1024
XATTN_TILE = 2048
SUB_TILE = 512
MIXER_SUB_TILE = 1024
MEM_BATCH = 4
VMEM_LIMIT_BYTES = 56 * 1024 * 1024

BF16 = jnp.bfloat16
F32 = jnp.float32

_NT = (((1,), (1,)), ((), ()))
_TN = (((0,), (0,)), ((), ()))


def _dot(a, b):
    return jnp.dot(a, b, preferred_element_type=F32)


def _rms(x, w):
    return x * lax.rsqrt(jnp.mean(x * x, axis=-1, keepdims=True) + EPS) * w


def _sigmoid(x):
    return 1.0 / (1.0 + jnp.exp(-x))


def _resident(shape):
    return pl.BlockSpec(shape, lambda *_: (0,) * len(shape), pipeline_mode=pl.Buffered(1))


def _mixer_kernel(x_ref, nw_ref, wp_ref, wqk_ref, wv_ref, wg_ref, wr_ref, poolw_ref, pscale_ref,
                  gkw2_ref, gkb_ref, gnw_ref, wout_ref, o_ref,
                  pbuf, gbuf, state, q_s, k_s, v_s, rs_s, cat_s, *, ts, sub):
    s = pl.program_id(1)

    @pl.when(s == 0)
    def _():
        pbuf[:, :POOL_HALO, :] = jnp.zeros((POOL_GROUPS, POOL_HALO, LANES), F32)
        state[...] = jnp.zeros_like(state)

    row = lax.broadcasted_iota(jnp.int32, (sub, LANES), 0)
    crow = row % GLA_CHUNK
    lane = lax.broadcasted_iota(jnp.int32, (1, LANES), 1)
    first_head = lane < GLA_DK
    ci = lax.broadcasted_iota(jnp.int32, (GLA_CHUNK, LANES), 0)
    cj = lax.broadcasted_iota(jnp.int32, (GLA_CHUNK, LANES), 1) % GLA_CHUNK
    causal2 = ci >= cj
    pairs = range(GLA_HEADS // 2)

    def by_head(a):
        return jnp.concatenate([jnp.where(first_head, a, 0.0), jnp.where(first_head, 0.0, a)], axis=0)

    def project(a):
        rows = slice(a, a + sub)
        h = _rms(x_ref[0, rows, :], nw_ref[...]).astype(BF16)
        g_low = _dot(h, wg_ref[...])
        p = _dot(h, wp_ref[...])
        gate = _dot(g_low.astype(BF16), gkw2_ref[...]) + gkb_ref[...]
        lg = (jnp.minimum(gate, 0.0) - jnp.log1p(jnp.exp(-jnp.abs(gate)))) * (1.0 / GLA_GATE_NORMALIZER)
        g0 = CUM_PAD + a
        for k in pairs:
            gb = gbuf.at[k]
            if a == 0:
                gb[:CUM_PAD, :] = jnp.zeros((CUM_PAD, LANES), F32)
            gb[g0:g0 + sub, :] = lg[:, k * LANES:(k + 1) * LANES]
        shift = 1
        while shift < GLA_CHUNK:
            for k in pairs:
                gb = gbuf.at[k]
                cur = gb[g0:g0 + sub, :]
                prev = gb[g0 - shift:g0 - shift + sub, :]
                gb[g0:g0 + sub, :] = cur + jnp.where(crow >= shift, prev, 0.0)
            shift *= 2

        qk = _dot(h, wqk_ref[...])
        q_s[rows, :] = qk[:, :GLA_KEY_WIDTH]
        k_s[rows, :] = qk[:, GLA_KEY_WIDTH:]
        v_s[rows, :] = _dot(h, wv_ref[...]).astype(BF16)
        r = _dot(h, wr_ref[...])
        rs_s[rows, :] = r * _sigmoid(r)

        t1 = s * ts + a + row + 1
        p0 = POOL_HALO + a
        for g, w in enumerate(POOL_WINDOWS):
            lo, hi = g * POOL_GROUP_DIM, (g + 1) * POOL_GROUP_DIM
            pg = p[:, lo:hi]
            buf = pbuf.at[g]
            buf[p0:p0 + sub, :] = pg
            win = pg
            for d in range(1, w):
                win = win + buf[p0 - d:p0 - d + sub, :]
            if a + sub == ts:
                buf[:POOL_HALO, :] = pg[sub - POOL_HALO:, :]
            cnt = jnp.minimum(t1, w).astype(F32)
            pooled = win / cnt - pg
            mixed = _dot(pooled.astype(BF16), poolw_ref[g])
            cat_s[rows, lo:hi] = (mixed * pscale_ref[:, lo:hi]).astype(BF16)

    def gla_products(a):
        out = {}
        for c in range(sub // GLA_CHUNK):
            r0 = a + c * GLA_CHUNK
            rows = slice(r0, r0 + GLA_CHUNK)
            for j in pairs:
                kl = slice(j * LANES, (j + 1) * LANES)
                Gc = gbuf[j, CUM_PAD + r0:CUM_PAD + r0 + GLA_CHUNK, :]
                qc = q_s[rows, kl]
                kc = k_s[rows, kl]
                g_last = Gc[GLA_CHUNK - 1:GLA_CHUNK, :]
                q_dec = qc * (GLA_DK ** -0.5) * jnp.exp(Gc)
                k_dec = by_head(kc * jnp.exp(-Gc)).astype(BF16)
                k_end = by_head(kc * jnp.exp(g_last - Gc)).astype(BF16)
                sc = lax.dot_general(q_dec.astype(BF16), k_dec, _NT, preferred_element_type=F32)
                v_rows = jnp.concatenate(
                    [v_s[rows, (2 * j + hh) * GLA_DV:(2 * j + hh + 1) * GLA_DV] for hh in range(2)], axis=0)
                out[j, c] = dict(
                    decay=jnp.exp(g_last),
                    q_rows=by_head(q_dec).astype(BF16),
                    sc_rows=by_head(jnp.where(causal2, sc, 0.0)).astype(BF16),
                    v_rows=v_rows,
                    kv=lax.dot_general(v_rows, k_end, _TN, preferred_element_type=F32))
        return out

    def gla_states(prod, st):
        st = list(st)
        for c in range(sub // GLA_CHUNK):
            for j in pairs:
                prod[j, c]["state"] = st[j]
                st[j] = st[j] * prod[j, c]["decay"] + prod[j, c]["kv"]
        return st

    def gla_outputs(a, prod):
        for c in range(sub // GLA_CHUNK):
            r0 = a + c * GLA_CHUNK
            rows = slice(r0, r0 + GLA_CHUNK)
            for j in pairs:
                pc = prod[j, c]
                lhs = jnp.concatenate([pc["sc_rows"], pc["q_rows"]], axis=1)
                rhs = jnp.concatenate([pc["v_rows"], pc["state"].T.astype(BF16)], axis=0)
                o2 = _dot(lhs, rhs)
                for hh in range(2):
                    hd = 2 * j + hh
                    vl = slice(hd * GLA_DV, (hd + 1) * GLA_DV)
                    o = o2[hh * GLA_CHUNK:(hh + 1) * GLA_CHUNK, :]
                    gl = _rms(o, gnw_ref[...]) * rs_s[rows, vl]
                    cat_s[rows, POOL_WIDTH + hd * GLA_DV:POOL_WIDTH + (hd + 1) * GLA_DV] = gl.astype(BF16)

    def output(a):
        rows = slice(a, a + sub)
        o_ref[0, rows, :] = x_ref[0, rows, :] + _dot(cat_s[rows, :], wout_ref[...])

    starts = list(range(0, ts, sub))
    st = [state[j] for j in pairs]
    project(starts[0])
    if len(starts) > 1:
        project(starts[1])
    prod = gla_products(starts[0])
    st = gla_states(prod, st)
    for i, a in enumerate(starts):
        gla_outputs(a, prod)
        if i + 1 < len(starts):
            if i + 2 < len(starts):
                project(starts[i + 2])
            prod = gla_products(starts[i + 1])
            st = gla_states(prod, st)
        output(a)
    for j in pairs:
        state[j] = st[j]


def _mixer(x, nw, wp, wqk, wv, wg, wr, poolw, pscale, gkw2, gkb, gnw, wout, *, ts):
    b, s, d = x.shape
    const = _resident
    return pl.pallas_call(
        functools.partial(_mixer_kernel, ts=ts, sub=min(ts, MIXER_SUB_TILE)),
        out_shape=jax.ShapeDtypeStruct(x.shape, F32),
        grid=(b, s // ts),
        in_specs=[
            pl.BlockSpec((1, ts, d), lambda i, j: (i, j, 0)),
            const(nw.shape), const(wp.shape), const(wqk.shape), const(wv.shape), const(wg.shape),
            const(wr.shape), const(poolw.shape), const(pscale.shape), const(gkw2.shape), const(gkb.shape),
            const(gnw.shape), const(wout.shape),
        ],
        out_specs=pl.BlockSpec((1, ts, d), lambda i, j: (i, j, 0)),
        scratch_shapes=[
            pltpu.VMEM((POOL_GROUPS, POOL_HALO + ts, LANES), F32),
            pltpu.VMEM((GLA_HEADS // 2, CUM_PAD + ts, LANES), F32),
            pltpu.VMEM((GLA_HEADS // 2, GLA_DV, LANES), F32),
            pltpu.VMEM((ts, GLA_KEY_WIDTH), F32),
            pltpu.VMEM((ts, GLA_KEY_WIDTH), F32),
            pltpu.VMEM((ts, GLA_VALUE_WIDTH), BF16),
            pltpu.VMEM((ts, GLA_VALUE_WIDTH), F32),
            pltpu.VMEM((ts, D_MODEL), BF16),
        ],
        compiler_params=pltpu.CompilerParams(
            dimension_semantics=("parallel", "arbitrary"), vmem_limit_bytes=VMEM_LIMIT_BYTES),
        name="mixer",
    )(x, nw, wp, wqk, wv, wg, wr, poolw, pscale, gkw2, gkb, gnw, wout)


def _mem_kv_kernel(mem_ref, nw_ref, wk_ref, wv_ref, kt_ref, v_ref):
    nb, m, d = mem_ref.shape
    h = _rms(mem_ref[...].reshape(nb * m, d), nw_ref[...]).astype(BF16)
    k = _dot(h, wk_ref[...])
    v_ref[...] = _dot(h, wv_ref[...]).astype(BF16).reshape(nb, m, d)
    for i in range(nb):
        kt_ref[i] = k[i * m:(i + 1) * m, :].T.astype(BF16)


def _mem_kv(mem, nw, wk, wv):
    b, m, d = mem.shape
    const = _resident
    return pl.pallas_call(
        _mem_kv_kernel,
        out_shape=(jax.ShapeDtypeStruct((b, d, m), BF16), jax.ShapeDtypeStruct((b, m, d), BF16)),
        grid=(b // MEM_BATCH,),
        in_specs=[pl.BlockSpec((MEM_BATCH, m, d), lambda i: (i, 0, 0)), const(nw.shape), const(wk.shape), const(wv.shape)],
        out_specs=(pl.BlockSpec((MEM_BATCH, d, m), lambda i: (i, 0, 0)), pl.BlockSpec((MEM_BATCH, m, d), lambda i: (i, 0, 0))),
        compiler_params=pltpu.CompilerParams(
            dimension_semantics=("parallel",), vmem_limit_bytes=VMEM_LIMIT_BYTES),
        name="mem_kv",
    )(mem, nw, wk, wv)


def _xattn_kernel(x_ref, nw_ref, wq_ref, kt_ref, v_ref, wo_ref, o_ref, *, ts, sub):
    half = D_MODEL // 2
    heads_per_half = XATTN_HEADS // 2

    def scores(a):
        h = _rms(x_ref[0, a:a + sub, :], nw_ref[...]).astype(BF16)
        out = []
        for g in range(2):
            q = (_dot(h, wq_ref[:, g * half:(g + 1) * half]) * (XATTN_HEAD_DIM ** -0.5)).astype(BF16)
            for i in range(heads_per_half):
                sl = slice(g * half + i * XATTN_HEAD_DIM, g * half + (i + 1) * XATTN_HEAD_DIM)
                out.append(_dot(q[:, i * XATTN_HEAD_DIM:(i + 1) * XATTN_HEAD_DIM], kt_ref[0, sl, :]))
        return out

    def attend(a, scs):
        y = x_ref[0, a:a + sub, :]
        for g in range(2):
            outs = []
            for i in range(heads_per_half):
                sc = scs[g * heads_per_half + i]
                sl = slice(g * half + i * XATTN_HEAD_DIM, g * half + (i + 1) * XATTN_HEAD_DIM)
                e = jnp.exp(sc - jnp.max(sc, axis=-1, keepdims=True))
                probs = e * (1.0 / jnp.sum(e, axis=-1, keepdims=True))
                outs.append(_dot(probs.astype(BF16), v_ref[0, :, sl]))
            o = jnp.concatenate(outs, axis=1).astype(BF16)
            y = y + _dot(o, wo_ref[g * half:(g + 1) * half, :])
        o_ref[0, a:a + sub, :] = y

    starts = list(range(0, ts, sub))
    scs = scores(starts[0])
    for i, a in enumerate(starts):
        nxt = scores(starts[i + 1]) if i + 1 < len(starts) else None
        attend(a, scs)
        scs = nxt


def _xattn(x, nw, wq, kt, v, wo, *, ts):
    b, s, d = x.shape
    const = _resident
    return pl.pallas_call(
        functools.partial(_xattn_kernel, ts=ts, sub=min(ts, SUB_TILE)),
        out_shape=jax.ShapeDtypeStruct(x.shape, F32),
        grid=(b, s // ts),
        in_specs=[
            pl.BlockSpec((1, ts, d), lambda i, j: (i, j, 0)),
            const(nw.shape), const(wq.shape),
            pl.BlockSpec((1, d, N_MEM), lambda i, j: (i, 0, 0)),
            pl.BlockSpec((1, N_MEM, d), lambda i, j: (i, 0, 0)),
            const(wo.shape),
        ],
        out_specs=pl.BlockSpec((1, ts, d), lambda i, j: (i, j, 0)),
        compiler_params=pltpu.CompilerParams(
            dimension_semantics=("parallel", "parallel"), vmem_limit_bytes=VMEM_LIMIT_BYTES),
        name="xattn",
    )(x, nw, wq, kt, v, wo)


def _ffn_kernel(x_ref, nw_ref, wup_ref, cw_ref, cb_ref, wdn_ref, nfw_ref, o_ref, ucarry, ubuf, act_s, *, ts):
    s = pl.program_id(1)

    @pl.when(s == 0)
    def _():
        ucarry[...] = jnp.zeros_like(ucarry)

    x = x_ref[0]
    h = _rms(x, nw_ref[...]).astype(BF16)

    tiles = 2 * FF_CHUNK // LANES

    def conv_pair(j):
        col0 = 2 * j * FF_CHUNK
        u = _dot(h, wup_ref[:, col0:col0 + 2 * FF_CHUNK])
        outs = []
        for k in range(tiles):
            cols = slice(col0 + k * LANES, col0 + (k + 1) * LANES)
            uk = u[:, k * LANES:(k + 1) * LANES]
            buf = ubuf.at[j % FF_UBUFS, k]
            buf[:SUBLANES, :] = ucarry[:, cols]
            buf[SUBLANES:, :] = uk
            ucarry[:, cols] = uk[ts - SUBLANES:, :]
            cw = cw_ref[:, cols]
            outs.append(cb_ref[:, cols] + cw[0:1, :] * buf[SUBLANES - 2:SUBLANES - 2 + ts, :]
                        + cw[1:2, :] * buf[SUBLANES - 1:SUBLANES - 1 + ts, :] + cw[2:3, :] * uk)
        return jnp.concatenate(outs[:tiles // 2], axis=1), jnp.concatenate(outs[tiles // 2:], axis=1)

    for j in range(D_FF // FF_CHUNK):
        gate, val = conv_pair(j)
        act_s[:, j * FF_CHUNK:(j + 1) * FF_CHUNK] = (gate * _sigmoid(gate) * val).astype(BF16)
    o_ref[0] = _rms(x + _dot(act_s[...], wdn_ref[...]), nfw_ref[...])


def _pair_columns(a):
    r = a.shape[0]
    return a.reshape(r, 2, D_FF // FF_CHUNK, FF_CHUNK).transpose(0, 2, 1, 3).reshape(r, 2 * D_FF)


def _ffn(x, nw, wup, cw, cb, wdn, nfw, *, ts):
    b, s, d = x.shape
    const = _resident
    return pl.pallas_call(
        functools.partial(_ffn_kernel, ts=ts),
        out_shape=jax.ShapeDtypeStruct(x.shape, F32),
        grid=(b, s // ts),
        in_specs=[
            pl.BlockSpec((1, ts, d), lambda i, j: (i, j, 0)),
            const(nw.shape), const(wup.shape), const(cw.shape), const(cb.shape), const(wdn.shape),
            const(nfw.shape),
        ],
        out_specs=pl.BlockSpec((1, ts, d), lambda i, j: (i, j, 0)),
        scratch_shapes=[
            pltpu.VMEM((SUBLANES, 2 * D_FF), F32),
            pltpu.VMEM((FF_UBUFS, 2 * FF_CHUNK // LANES, SUBLANES + ts, LANES), F32),
            pltpu.VMEM((ts, D_FF), BF16),
        ],
        compiler_params=pltpu.CompilerParams(
            dimension_semantics=("parallel", "arbitrary"), vmem_limit_bytes=VMEM_LIMIT_BYTES),
        name="ffn",
    )(x, nw, wup, cw, cb, wdn, nfw)


def kernel(x, mem, norm_mix_w, w_in, pool_w, pool_scale, gk_w2, gk_b, gla_norm_w, w_out, norm_xattn_w,
           norm_mem_w, xattn_wq, xattn_wkv, xattn_wo, norm_ffn_w, ffn_w_up, ffn_conv_w, ffn_conv_b,
           ffn_w_down, norm_final_w):
    depth = w_in.shape[0]
    off_q = POOL_WIDTH
    off_v = off_q + 2 * GLA_KEY_WIDTH
    off_g = off_v + GLA_VALUE_WIDTH
    off_r = off_g + GLA_GATE_RANK
    row2 = lambda a: a.reshape(1, -1)
    for l in range(depth):
        wi = w_in[l]
        wg = jnp.pad(wi[:, off_g:off_r], ((0, 0), (0, LANES - GLA_GATE_RANK))).astype(BF16)
        gkw2 = jnp.pad(gk_w2[l], ((0, LANES - GLA_GATE_RANK), (0, 0))).astype(BF16)
        x = _mixer(
            x, row2(norm_mix_w[l]), wi[:, :off_q].astype(BF16), wi[:, off_q:off_v].astype(BF16),
            wi[:, off_v:off_g].astype(BF16), wg, wi[:, off_r:].astype(BF16), pool_w[l].astype(BF16),
            row2(pool_scale[l]), gkw2, row2(gk_b[l]), row2(gla_norm_w[l]), w_out[l].astype(BF16), ts=SEQ_TILE)
        wkv = xattn_wkv[l]
        kt, v = _mem_kv(mem, row2(norm_mem_w[l]), wkv[:, :D_MODEL].astype(BF16), wkv[:, D_MODEL:].astype(BF16))
        x = _xattn(x, row2(norm_xattn_w[l]), xattn_wq[l].astype(BF16), kt, v, xattn_wo[l].astype(BF16), ts=XATTN_TILE)
        assert depth == 1
        x = _ffn(x, row2(norm_ffn_w[l]), _pair_columns(ffn_w_up[l]).astype(BF16), _pair_columns(ffn_conv_w[l]),
                 _pair_columns(row2(ffn_conv_b[l])), ffn_w_down[l].astype(BF16), row2(norm_final_w), ts=SEQ_TILE)
    return x
```

```python
import functools

import jax
import jax.numpy as jnp
from jax import lax
from jax.experimental import pallas as pl
from jax.experimental.pallas import tpu as pltpu

D_MODEL = 1024
N_MEM = 256
POOL_WIDTH = 512
POOL_GROUPS = 4
POOL_GROUP_DIM = 128
POOL_WINDOWS = (2, 4, 8, 16)
POOL_HALO = 16
GLA_VALUE_WIDTH = 512
GLA_KEY_WIDTH = 256
GLA_HEADS = 4
GLA_DK = 64
GLA_DV = 128
GLA_GATE_RANK = 16
CUM_PAD = 32
GLA_GATE_NORMALIZER = 16.0
GLA_CHUNK = 64
XATTN_HEADS = 4
XATTN_HEAD_DIM = 256
D_FF = 2816
CONV_WIDTH = 3
EPS = 1e-6

LANES = 128
SUBLANES = 8
FF_CHUNK = 256
FF_UBUFS = 4
SEQ_TILE = 1024
XATTN_TILE = 2048
SUB_TILE = 512
MIXER_SUB_TILE = 1024
MEM_BATCH = 2
VMEM_LIMIT_BYTES = 56 * 1024 * 1024

BF16 = jnp.bfloat16
F32 = jnp.float32

_NT = (((1,), (1,)), ((), ()))
_TN = (((0,), (0,)), ((), ()))


def _dot(a, b):
    return jnp.dot(a, b, preferred_element_type=F32)


def _rms(x, w):
    return x * lax.rsqrt(jnp.mean(x * x, axis=-1, keepdims=True) + EPS) * w


def _sigmoid(x):
    return 1.0 / (1.0 + jnp.exp(-x))


def _resident(shape):
    return pl.BlockSpec(shape, lambda *_: (0,) * len(shape), pipeline_mode=pl.Buffered(1))


def _mixer_kernel(x_ref, nw_ref, wp_ref, wqk_ref, wv_ref, wg_ref, wr_ref,
                  gkw2_ref, gkb_ref, gnw_ref, wout_ref, o_ref,
                  pbuf, gbuf, state, q_s, k_s, v_s, rs_s, cat_s, *, ts, sub):
    s = pl.program_id(1)

    @pl.when(s == 0)
    def _():
        pbuf[:, :POOL_HALO, :] = jnp.zeros((POOL_GROUPS, POOL_HALO, LANES), F32)
        state[...] = jnp.zeros_like(state)

    row = lax.broadcasted_iota(jnp.int32, (sub, LANES), 0)
    crow = row % GLA_CHUNK
    lane = lax.broadcasted_iota(jnp.int32, (1, LANES), 1)
    first_head = lane < GLA_DK
    ci = lax.broadcasted_iota(jnp.int32, (GLA_CHUNK, LANES), 0)
    cj = lax.broadcasted_iota(jnp.int32, (GLA_CHUNK, LANES), 1) % GLA_CHUNK
    causal2 = ci >= cj
    pairs = range(GLA_HEADS // 2)

    def by_head(a):
        return jnp.concatenate([jnp.where(first_head, a, 0.0), jnp.where(first_head, 0.0, a)], axis=0)

    def project(a):
        rows = slice(a, a + sub)
        h = _rms(x_ref[0, rows, :], nw_ref[...]).astype(BF16)
        g_low = _dot(h, wg_ref[...])
        p = _dot(h, wp_ref[...])
        gate = _dot(g_low.astype(BF16), gkw2_ref[...]) + gkb_ref[...]
        lg = (jnp.minimum(gate, 0.0) - jnp.log1p(jnp.exp(-jnp.abs(gate)))) * (1.0 / GLA_GATE_NORMALIZER)
        g0 = CUM_PAD + a
        for k in pairs:
            gb = gbuf.at[k]
            if a == 0:
                gb[:CUM_PAD, :] = jnp.zeros((CUM_PAD, LANES), F32)
            gb[g0:g0 + sub, :] = lg[:, k * LANES:(k + 1) * LANES]
        shift = 1
        while shift < GLA_CHUNK:
            for k in pairs:
                gb = gbuf.at[k]
                cur = gb[g0:g0 + sub, :]
                prev = gb[g0 - shift:g0 - shift + sub, :]
                gb[g0:g0 + sub, :] = cur + jnp.where(crow >= shift, prev, 0.0)
            shift *= 2

        qk = _dot(h, wqk_ref[...])
        q_s[rows, :] = qk[:, :GLA_KEY_WIDTH]
        k_s[rows, :] = qk[:, GLA_KEY_WIDTH:]
        v_s[rows, :] = _dot(h, wv_ref[...]).astype(BF16)
        r = _dot(h, wr_ref[...])
        rs_s[rows, :] = r * _sigmoid(r)

        t1 = s * ts + a + row + 1
        p0 = POOL_HALO + a
        for g, w in enumerate(POOL_WINDOWS):
            lo, hi = g * POOL_GROUP_DIM, (g + 1) * POOL_GROUP_DIM
            pg = p[:, lo:hi]
            buf = pbuf.at[g]
            buf[p0:p0 + sub, :] = pg
            win = pg
            for d in range(1, w):
                win = win + buf[p0 - d:p0 - d + sub, :]
            if a + sub == ts:
                buf[:POOL_HALO, :] = pg[sub - POOL_HALO:, :]
            cnt = jnp.minimum(t1, w).astype(F32)
            cat_s[rows, lo:hi] = (win / cnt - pg).astype(BF16)

    def gla_products(a):
        out = {}
        for c in range(sub // GLA_CHUNK):
            r0 = a + c * GLA_CHUNK
            rows = slice(r0, r0 + GLA_CHUNK)
            for j in pairs:
                kl = slice(j * LANES, (j + 1) * LANES)
                Gc = gbuf[j, CUM_PAD + r0:CUM_PAD + r0 + GLA_CHUNK, :]
                qc = q_s[rows, kl]
                kc = k_s[rows, kl]
                g_last = Gc[GLA_CHUNK - 1:GLA_CHUNK, :]
                q_dec = qc * (GLA_DK ** -0.5) * jnp.exp(Gc)
                k_dec = by_head(kc * jnp.exp(-Gc)).astype(BF16)
                k_end = by_head(kc * jnp.exp(g_last - Gc)).astype(BF16)
                sc = lax.dot_general(q_dec.astype(BF16), k_dec, _NT, preferred_element_type=F32)
                v_rows = jnp.concatenate(
                    [v_s[rows, (2 * j + hh) * GLA_DV:(2 * j + hh + 1) * GLA_DV] for hh in range(2)], axis=0)
                out[j, c] = dict(
                    decay=jnp.exp(g_last),
                    q_rows=by_head(q_dec).astype(BF16),
                    sc_rows=by_head(jnp.where(causal2, sc, 0.0)).astype(BF16),
                    v_rows=v_rows,
                    kv=lax.dot_general(v_rows, k_end, _TN, preferred_element_type=F32))
        return out

    def gla_states(prod, st):
        st = list(st)
        for c in range(sub // GLA_CHUNK):
            for j in pairs:
                prod[j, c]["state"] = st[j]
                st[j] = st[j] * prod[j, c]["decay"] + prod[j, c]["kv"]
        return st

    def gla_outputs(a, prod):
        for c in range(sub // GLA_CHUNK):
            r0 = a + c * GLA_CHUNK
            rows = slice(r0, r0 + GLA_CHUNK)
            for j in pairs:
                pc = prod[j, c]
                lhs = jnp.concatenate([pc["sc_rows"], pc["q_rows"]], axis=1)
                rhs = jnp.concatenate([pc["v_rows"], pc["state"].T.astype(BF16)], axis=0)
                o2 = _dot(lhs, rhs)
                for hh in range(2):
                    hd = 2 * j + hh
                    vl = slice(hd * GLA_DV, (hd + 1) * GLA_DV)
                    o = o2[hh * GLA_CHUNK:(hh + 1) * GLA_CHUNK, :]
                    gl = _rms(o, gnw_ref[...]) * rs_s[rows, vl]
                    cat_s[rows, POOL_WIDTH + hd * GLA_DV:POOL_WIDTH + (hd + 1) * GLA_DV] = gl.astype(BF16)

    def output(a):
        rows = slice(a, a + sub)
        o_ref[0, rows, :] = x_ref[0, rows, :] + _dot(cat_s[rows, :], wout_ref[...])

    starts = list(range(0, ts, sub))
    st = [state[j] for j in pairs]
    project(starts[0])
    if len(starts) > 1:
        project(starts[1])
    prod = gla_products(starts[0])
    st = gla_states(prod, st)
    for i, a in enumerate(starts):
        gla_outputs(a, prod)
        if i + 1 < len(starts):
            if i + 2 < len(starts):
                project(starts[i + 2])
            prod = gla_products(starts[i + 1])
            st = gla_states(prod, st)
        output(a)
    for j in pairs:
        state[j] = st[j]


def _fold_pool_kernel(poolw_ref, pscale_ref, wout_ref, o_ref):
    for g in range(POOL_GROUPS):
        lo, hi = g * POOL_GROUP_DIM, (g + 1) * POOL_GROUP_DIM
        pw = (poolw_ref[g] * pscale_ref[:, lo:hi]).astype(BF16)
        o_ref[lo:hi, :] = _dot(pw, wout_ref[lo:hi, :].astype(BF16)).astype(BF16)
    o_ref[POOL_WIDTH:, :] = wout_ref[POOL_WIDTH:, :].astype(BF16)


def _fold_pool(poolw, pscale, wout):
    return pl.pallas_call(
        _fold_pool_kernel,
        out_shape=jax.ShapeDtypeStruct(wout.shape, BF16),
        name="fold_pool",
    )(poolw, pscale, wout)


def _mixer(x, nw, wp, wqk, wv, wg, wr, gkw2, gkb, gnw, wout, *, ts):
    b, s, d = x.shape
    const = _resident
    return pl.pallas_call(
        functools.partial(_mixer_kernel, ts=ts, sub=min(ts, MIXER_SUB_TILE)),
        out_shape=jax.ShapeDtypeStruct(x.shape, F32),
        grid=(b, s // ts),
        in_specs=[
            pl.BlockSpec((1, ts, d), lambda i, j: (i, j, 0)),
            const(nw.shape), const(wp.shape), const(wqk.shape), const(wv.shape), const(wg.shape),
            const(wr.shape), const(gkw2.shape), const(gkb.shape),
            const(gnw.shape), const(wout.shape),
        ],
        out_specs=pl.BlockSpec((1, ts, d), lambda i, j: (i, j, 0)),
        scratch_shapes=[
            pltpu.VMEM((POOL_GROUPS, POOL_HALO + ts, LANES), F32),
            pltpu.VMEM((GLA_HEADS // 2, CUM_PAD + ts, LANES), F32),
            pltpu.VMEM((GLA_HEADS // 2, GLA_DV, LANES), F32),
            pltpu.VMEM((ts, GLA_KEY_WIDTH), F32),
            pltpu.VMEM((ts, GLA_KEY_WIDTH), F32),
            pltpu.VMEM((ts, GLA_VALUE_WIDTH), BF16),
            pltpu.VMEM((ts, GLA_VALUE_WIDTH), F32),
            pltpu.VMEM((ts, D_MODEL), BF16),
        ],
        compiler_params=pltpu.CompilerParams(
            dimension_semantics=("parallel", "arbitrary"), vmem_limit_bytes=VMEM_LIMIT_BYTES),
        name="mixer",
    )(x, nw, wp, wqk, wv, wg, wr, gkw2, gkb, gnw, wout)


def _mem_weights_kernel(mem_ref, nw_ref, wk_ref, wv_ref, wq_ref, wo_ref, wqk_ref, vwo_ref):
    nb, m, d = mem_ref.shape
    h = _rms(mem_ref[...].reshape(nb * m, d), nw_ref[...]).astype(BF16)
    k = _dot(h, wk_ref[...]).astype(BF16)
    v = _dot(h, wv_ref[...]).astype(BF16)
    for i in range(nb):
        for hd in range(XATTN_HEADS):
            sl = slice(hd * XATTN_HEAD_DIM, (hd + 1) * XATTN_HEAD_DIM)
            ml = slice(hd * m, (hd + 1) * m)
            kh = k[i * m:(i + 1) * m, sl]
            vh = v[i * m:(i + 1) * m, sl]
            wqk = lax.dot_general(wq_ref[:, sl], kh, _NT, preferred_element_type=F32)
            wqk_ref[i, :, ml] = (wqk * (XATTN_HEAD_DIM ** -0.5)).astype(BF16)
            vwo_ref[i, ml, :] = _dot(vh, wo_ref[sl, :]).astype(BF16)


def _mem_weights(mem, nw, wk, wv, wq, wo):
    b, m, d = mem.shape
    hm = XATTN_HEADS * m
    const = _resident
    return pl.pallas_call(
        _mem_weights_kernel,
        out_shape=(jax.ShapeDtypeStruct((b, d, hm), BF16), jax.ShapeDtypeStruct((b, hm, d), BF16)),
        grid=(b // MEM_BATCH,),
        in_specs=[pl.BlockSpec((MEM_BATCH, m, d), lambda i: (i, 0, 0)), const(nw.shape), const(wk.shape),
                  const(wv.shape), const(wq.shape), const(wo.shape)],
        out_specs=(pl.BlockSpec((MEM_BATCH, d, hm), lambda i: (i, 0, 0)),
                   pl.BlockSpec((MEM_BATCH, hm, d), lambda i: (i, 0, 0))),
        compiler_params=pltpu.CompilerParams(
            dimension_semantics=("parallel",), vmem_limit_bytes=VMEM_LIMIT_BYTES),
        name="mem_weights",
    )(mem, nw, wk, wv, wq, wo)


def _xattn_kernel(x_ref, nw_ref, wqk_ref, vwo_ref, o_ref, *, ts, sub):
    n_mem = wqk_ref.shape[2] // XATTN_HEADS

    def scores(a):
        h = _rms(x_ref[0, a:a + sub, :], nw_ref[...]).astype(BF16)
        return _dot(h, wqk_ref[0])

    def attend(a, sc):
        parts = []
        for hd in range(XATTN_HEADS):
            s = sc[:, hd * n_mem:(hd + 1) * n_mem]
            e = jnp.exp(s - jnp.max(s, axis=-1, keepdims=True))
            parts.append((e * (1.0 / jnp.sum(e, axis=-1, keepdims=True))).astype(BF16))
        probs = jnp.concatenate(parts, axis=1)
        o_ref[0, a:a + sub, :] = x_ref[0, a:a + sub, :] + _dot(probs, vwo_ref[0])

    starts = list(range(0, ts, sub))
    sc = scores(starts[0])
    for i, a in enumerate(starts):
        nxt = scores(starts[i + 1]) if i + 1 < len(starts) else None
        attend(a, sc)
        sc = nxt


def _xattn(x, nw, wqk, vwo, *, ts):
    b, s, d = x.shape
    const = _resident
    return pl.pallas_call(
        functools.partial(_xattn_kernel, ts=ts, sub=min(ts, SUB_TILE)),
        out_shape=jax.ShapeDtypeStruct(x.shape, F32),
        grid=(b, s // ts),
        in_specs=[
            pl.BlockSpec((1, ts, d), lambda i, j: (i, j, 0)),
            const(nw.shape),
            pl.BlockSpec((1,) + wqk.shape[1:], lambda i, j: (i, 0, 0)),
            pl.BlockSpec((1,) + vwo.shape[1:], lambda i, j: (i, 0, 0)),
        ],
        out_specs=pl.BlockSpec((1, ts, d), lambda i, j: (i, j, 0)),
        compiler_params=pltpu.CompilerParams(
            dimension_semantics=("parallel", "parallel"), vmem_limit_bytes=VMEM_LIMIT_BYTES),
        name="xattn",
    )(x, nw, wqk, vwo)


def _ffn_kernel(x_ref, nw_ref, wup_ref, cw_ref, cb_ref, wdn_ref, nfw_ref, o_ref, ucarry, ubuf, act_s, *, ts):
    s = pl.program_id(1)

    @pl.when(s == 0)
    def _():
        ucarry[...] = jnp.zeros_like(ucarry)

    x = x_ref[0]
    h = _rms(x, nw_ref[...]).astype(BF16)

    def conv_cols(col0, slot):
        u = _dot(h, wup_ref[:, col0:col0 + FF_CHUNK])
        outs = []
        for k in range(FF_CHUNK // LANES):
            cols = slice(col0 + k * LANES, col0 + (k + 1) * LANES)
            uk = u[:, k * LANES:(k + 1) * LANES]
            buf = ubuf.at[slot, k]
            buf[:SUBLANES, :] = ucarry[:, cols]
            buf[SUBLANES:, :] = uk
            ucarry[:, cols] = uk[ts - SUBLANES:, :]
            cw = cw_ref[:, cols]
            outs.append(cb_ref[:, cols] + cw[0:1, :] * buf[SUBLANES - 2:SUBLANES - 2 + ts, :]
                        + cw[1:2, :] * buf[SUBLANES - 1:SUBLANES - 1 + ts, :] + cw[2:3, :] * uk)
        return jnp.concatenate(outs, axis=1)

    for j in range(D_FF // FF_CHUNK):
        gate = conv_cols(j * FF_CHUNK, (2 * j) % FF_UBUFS)
        val = conv_cols(D_FF + j * FF_CHUNK, (2 * j + 1) % FF_UBUFS)
        act_s[:, j * FF_CHUNK:(j + 1) * FF_CHUNK] = (gate * _sigmoid(gate) * val).astype(BF16)
    o_ref[0] = _rms(x + _dot(act_s[...], wdn_ref[...]), nfw_ref[...])


def _ffn(x, nw, wup, cw, cb, wdn, nfw, *, ts):
    b, s, d = x.shape
    const = _resident
    return pl.pallas_call(
        functools.partial(_ffn_kernel, ts=ts),
        out_shape=jax.ShapeDtypeStruct(x.shape, F32),
        grid=(b, s // ts),
        in_specs=[
            pl.BlockSpec((1, ts, d), lambda i, j: (i, j, 0)),
            const(nw.shape), const(wup.shape), const(cw.shape), const(cb.shape), const(wdn.shape),
            const(nfw.shape),
        ],
        out_specs=pl.BlockSpec((1, ts, d), lambda i, j: (i, j, 0)),
        scratch_shapes=[
            pltpu.VMEM((SUBLANES, 2 * D_FF), F32),
            pltpu.VMEM((FF_UBUFS, FF_CHUNK // LANES, SUBLANES + ts, LANES), F32),
            pltpu.VMEM((ts, D_FF), BF16),
        ],
        compiler_params=pltpu.CompilerParams(
            dimension_semantics=("parallel", "arbitrary"), vmem_limit_bytes=VMEM_LIMIT_BYTES),
        name="ffn",
    )(x, nw, wup, cw, cb, wdn, nfw)


def kernel(x, mem, norm_mix_w, w_in, pool_w, pool_scale, gk_w2, gk_b, gla_norm_w, w_out, norm_xattn_w,
           norm_mem_w, xattn_wq, xattn_wkv, xattn_wo, norm_ffn_w, ffn_w_up, ffn_conv_w, ffn_conv_b,
           ffn_w_down, norm_final_w):
    depth = w_in.shape[0]
    off_q = POOL_WIDTH
    off_v = off_q + 2 * GLA_KEY_WIDTH
    off_g = off_v + GLA_VALUE_WIDTH
    off_r = off_g + GLA_GATE_RANK
    row2 = lambda a: a.reshape(1, -1)
    for l in range(depth):
        wi = w_in[l]
        wg = jnp.pad(wi[:, off_g:off_r], ((0, 0), (0, LANES - GLA_GATE_RANK))).astype(BF16)
        gkw2 = jnp.pad(gk_w2[l], ((0, LANES - GLA_GATE_RANK), (0, 0))).astype(BF16)
        x = _mixer(
            x, row2(norm_mix_w[l]), wi[:, :off_q].astype(BF16), wi[:, off_q:off_v].astype(BF16),
            wi[:, off_v:off_g].astype(BF16), wg, wi[:, off_r:].astype(BF16),
            gkw2, row2(gk_b[l]), row2(gla_norm_w[l]), _fold_pool(pool_w[l], row2(pool_scale[l]), w_out[l]), ts=SEQ_TILE)
        wkv = xattn_wkv[l]
        wqk, vwo = _mem_weights(mem, row2(norm_mem_w[l]), wkv[:, :D_MODEL].astype(BF16), wkv[:, D_MODEL:].astype(BF16),
                                xattn_wq[l].astype(BF16), xattn_wo[l].astype(BF16))
        x = _xattn(x, row2(norm_xattn_w[l]), wqk, vwo, ts=XATTN_TILE)
        assert depth == 1
        x = _ffn(x, row2(norm_ffn_w[l]), ffn_w_up[l].astype(BF16), ffn_conv_w[l], row2(ffn_conv_b[l]),
                 ffn_w_down[l].astype(BF16), row2(norm_final_w), ts=SEQ_TILE)
    return x
```

```python
import functools

import jax
import jax.numpy as jnp
from jax import lax
from jax.experimental import pallas as pl
from jax.experimental.pallas import tpu as pltpu

D_MODEL = 1024
N_MEM = 256
POOL_WIDTH = 512
POOL_GROUPS = 4
POOL_GROUP_DIM = 128
POOL_WINDOWS = (2, 4, 8, 16)
POOL_HALO = 16
GLA_VALUE_WIDTH = 512
GLA_KEY_WIDTH = 256
GLA_HEADS = 4
GLA_DK = 64
GLA_DV = 128
GLA_GATE_RANK = 16
CUM_PAD = 32
GLA_GATE_NORMALIZER = 16.0
GLA_CHUNK = 64
XATTN_HEADS = 4
XATTN_HEAD_DIM = 256
D_FF = 2816
CONV_WIDTH = 3
EPS = 1e-6

LANES = 128
SUBLANES = 8
FF_CHUNK = 256
FF_UBUFS = 4
SEQ_TILE = 1024
XATTN_TILE = 2048
SUB_TILE = 1024
MIXER_SUB_TILE = 1024
MEM_BATCH = 2
VMEM_LIMIT_BYTES = 56 * 1024 * 1024

BF16 = jnp.bfloat16
F32 = jnp.float32

_NT = (((1,), (1,)), ((), ()))
_TN = (((0,), (0,)), ((), ()))


def _dot(a, b):
    return jnp.dot(a, b, preferred_element_type=F32)


def _rms(x, w):
    return x * lax.rsqrt(jnp.mean(x * x, axis=-1, keepdims=True) + EPS) * w


def _sigmoid(x):
    return 1.0 / (1.0 + jnp.exp(-x))


def _resident(shape):
    return pl.BlockSpec(shape, lambda *_: (0,) * len(shape), pipeline_mode=pl.Buffered(1))


def _mixer_kernel(x_ref, nw_ref, wp_ref, wqk_ref, wv_ref, wg_ref, wr_ref,
                  gkw2_ref, gkb_ref, gnw_ref, wout_ref, o_ref,
                  pbuf, gbuf, state, q_s, k_s, v_s, rs_s, cat_s, *, ts, sub):
    s = pl.program_id(1)

    @pl.when(s == 0)
    def _():
        pbuf[:, :POOL_HALO, :] = jnp.zeros((POOL_GROUPS, POOL_HALO, LANES), F32)
        state[...] = jnp.zeros_like(state)

    row = lax.broadcasted_iota(jnp.int32, (sub, LANES), 0)
    crow = row % GLA_CHUNK
    lane = lax.broadcasted_iota(jnp.int32, (1, LANES), 1)
    first_head = lane < GLA_DK
    ci = lax.broadcasted_iota(jnp.int32, (GLA_CHUNK, LANES), 0)
    cj = lax.broadcasted_iota(jnp.int32, (GLA_CHUNK, LANES), 1) % GLA_CHUNK
    causal2 = ci >= cj
    pairs = range(GLA_HEADS // 2)

    def by_head(a):
        return jnp.concatenate([jnp.where(first_head, a, 0.0), jnp.where(first_head, 0.0, a)], axis=0)

    def project(a):
        rows = slice(a, a + sub)
        h = _rms(x_ref[0, rows, :], nw_ref[...]).astype(BF16)
        g_low = _dot(h, wg_ref[...])
        p = _dot(h, wp_ref[...])
        gate = _dot(g_low.astype(BF16), gkw2_ref[...]) + gkb_ref[...]
        lg = (jnp.minimum(gate, 0.0) - jnp.log1p(jnp.exp(-jnp.abs(gate)))) * (1.0 / GLA_GATE_NORMALIZER)
        g0 = CUM_PAD + a
        for k in pairs:
            gb = gbuf.at[k]
            if a == 0:
                gb[:CUM_PAD, :] = jnp.zeros((CUM_PAD, LANES), F32)
            gb[g0:g0 + sub, :] = lg[:, k * LANES:(k + 1) * LANES]
        shift = 1
        while shift < GLA_CHUNK:
            for k in pairs:
                gb = gbuf.at[k]
                cur = gb[g0:g0 + sub, :]
                prev = gb[g0 - shift:g0 - shift + sub, :]
                gb[g0:g0 + sub, :] = cur + jnp.where(crow >= shift, prev, 0.0)
            shift *= 2

        qk = _dot(h, wqk_ref[...])
        q_s[rows, :] = qk[:, :GLA_KEY_WIDTH]
        k_s[rows, :] = qk[:, GLA_KEY_WIDTH:]
        v_s[rows, :] = _dot(h, wv_ref[...]).astype(BF16)
        r = _dot(h, wr_ref[...])
        rs_s[rows, :] = r * _sigmoid(r)

        t1 = s * ts + a + row + 1
        p0 = POOL_HALO + a
        for g, w in enumerate(POOL_WINDOWS):
            lo, hi = g * POOL_GROUP_DIM, (g + 1) * POOL_GROUP_DIM
            pg = p[:, lo:hi]
            buf = pbuf.at[g]
            buf[p0:p0 + sub, :] = pg
            win = pg
            for d in range(1, w):
                win = win + buf[p0 - d:p0 - d + sub, :]
            if a + sub == ts:
                buf[:POOL_HALO, :] = pg[sub - POOL_HALO:, :]
            cnt = jnp.minimum(t1, w).astype(F32)
            cat_s[rows, lo:hi] = (win / cnt - pg).astype(BF16)

    def gla_products(a):
        out = {}
        for c in range(sub // GLA_CHUNK):
            r0 = a + c * GLA_CHUNK
            rows = slice(r0, r0 + GLA_CHUNK)
            for j in pairs:
                kl = slice(j * LANES, (j + 1) * LANES)
                Gc = gbuf[j, CUM_PAD + r0:CUM_PAD + r0 + GLA_CHUNK, :]
                qc = q_s[rows, kl]
                kc = k_s[rows, kl]
                g_last = Gc[GLA_CHUNK - 1:GLA_CHUNK, :]
                q_dec = qc * (GLA_DK ** -0.5) * jnp.exp(Gc)
                k_dec = by_head(kc * jnp.exp(-Gc)).astype(BF16)
                k_end = by_head(kc * jnp.exp(g_last - Gc)).astype(BF16)
                sc = lax.dot_general(q_dec.astype(BF16), k_dec, _NT, preferred_element_type=F32)
                v_rows = jnp.concatenate(
                    [v_s[rows, (2 * j + hh) * GLA_DV:(2 * j + hh + 1) * GLA_DV] for hh in range(2)], axis=0)
                out[j, c] = dict(
                    decay=jnp.exp(g_last),
                    q_rows=by_head(q_dec).astype(BF16),
                    sc_rows=by_head(jnp.where(causal2, sc, 0.0)).astype(BF16),
                    v_rows=v_rows,
                    kv=lax.dot_general(v_rows, k_end, _TN, preferred_element_type=F32))
        return out

    def gla_states(prod, st):
        st = list(st)
        for c in range(sub // GLA_CHUNK):
            for j in pairs:
                prod[j, c]["state"] = st[j]
                st[j] = st[j] * prod[j, c]["decay"] + prod[j, c]["kv"]
        return st

    def gla_outputs(a, prod):
        for c in range(sub // GLA_CHUNK):
            r0 = a + c * GLA_CHUNK
            rows = slice(r0, r0 + GLA_CHUNK)
            for j in pairs:
                pc = prod[j, c]
                lhs = jnp.concatenate([pc["sc_rows"], pc["q_rows"]], axis=1)
                rhs = jnp.concatenate([pc["v_rows"], pc["state"].T.astype(BF16)], axis=0)
                o2 = _dot(lhs, rhs)
                for hh in range(2):
                    hd = 2 * j + hh
                    vl = slice(hd * GLA_DV, (hd + 1) * GLA_DV)
                    o = o2[hh * GLA_CHUNK:(hh + 1) * GLA_CHUNK, :]
                    gl = _rms(o, gnw_ref[...]) * rs_s[rows, vl]
                    cat_s[rows, POOL_WIDTH + hd * GLA_DV:POOL_WIDTH + (hd + 1) * GLA_DV] = gl.astype(BF16)

    def output(a):
        rows = slice(a, a + sub)
        o_ref[0, rows, :] = x_ref[0, rows, :] + _dot(cat_s[rows, :], wout_ref[...])

    starts = list(range(0, ts, sub))
    st = [state[j] for j in pairs]
    project(starts[0])
    if len(starts) > 1:
        project(starts[1])
    prod = gla_products(starts[0])
    st = gla_states(prod, st)
    for i, a in enumerate(starts):
        gla_outputs(a, prod)
        if i + 1 < len(starts):
            if i + 2 < len(starts):
                project(starts[i + 2])
            prod = gla_products(starts[i + 1])
            st = gla_states(prod, st)
        output(a)
    for j in pairs:
        state[j] = st[j]


def _fold_pool_kernel(poolw_ref, pscale_ref, wout_ref, o_ref):
    for g in range(POOL_GROUPS):
        lo, hi = g * POOL_GROUP_DIM, (g + 1) * POOL_GROUP_DIM
        pw = (poolw_ref[g] * pscale_ref[:, lo:hi]).astype(BF16)
        o_ref[lo:hi, :] = _dot(pw, wout_ref[lo:hi, :].astype(BF16)).astype(BF16)
    o_ref[POOL_WIDTH:, :] = wout_ref[POOL_WIDTH:, :].astype(BF16)


def _fold_pool(poolw, pscale, wout):
    return pl.pallas_call(
        _fold_pool_kernel,
        out_shape=jax.ShapeDtypeStruct(wout.shape, BF16),
        name="fold_pool",
    )(poolw, pscale, wout)


def _mixer(x, nw, wp, wqk, wv, wg, wr, gkw2, gkb, gnw, wout, *, ts):
    b, s, d = x.shape
    const = _resident
    return pl.pallas_call(
        functools.partial(_mixer_kernel, ts=ts, sub=min(ts, MIXER_SUB_TILE)),
        out_shape=jax.ShapeDtypeStruct(x.shape, F32),
        grid=(b, s // ts),
        in_specs=[
            pl.BlockSpec((1, ts, d), lambda i, j: (i, j, 0)),
            const(nw.shape), const(wp.shape), const(wqk.shape), const(wv.shape), const(wg.shape),
            const(wr.shape), const(gkw2.shape), const(gkb.shape),
            const(gnw.shape), const(wout.shape),
        ],
        out_specs=pl.BlockSpec((1, ts, d), lambda i, j: (i, j, 0)),
        scratch_shapes=[
            pltpu.VMEM((POOL_GROUPS, POOL_HALO + ts, LANES), F32),
            pltpu.VMEM((GLA_HEADS // 2, CUM_PAD + ts, LANES), F32),
            pltpu.VMEM((GLA_HEADS // 2, GLA_DV, LANES), F32),
            pltpu.VMEM((ts, GLA_KEY_WIDTH), F32),
            pltpu.VMEM((ts, GLA_KEY_WIDTH), F32),
            pltpu.VMEM((ts, GLA_VALUE_WIDTH), BF16),
            pltpu.VMEM((ts, GLA_VALUE_WIDTH), F32),
            pltpu.VMEM((ts, D_MODEL), BF16),
        ],
        compiler_params=pltpu.CompilerParams(
            dimension_semantics=("parallel", "arbitrary"), vmem_limit_bytes=VMEM_LIMIT_BYTES),
        name="mixer",
    )(x, nw, wp, wqk, wv, wg, wr, gkw2, gkb, gnw, wout)


def _mem_weights_kernel(mem_ref, nw_ref, wk_ref, wv_ref, wq_ref, wo_ref, wqk_ref, vwo_ref):
    nb, m, d = mem_ref.shape
    h = _rms(mem_ref[...].reshape(nb * m, d), nw_ref[...]).astype(BF16)
    k = _dot(h, wk_ref[...]).astype(BF16)
    v = _dot(h, wv_ref[...]).astype(BF16)
    for i in range(nb):
        for hd in range(XATTN_HEADS):
            sl = slice(hd * XATTN_HEAD_DIM, (hd + 1) * XATTN_HEAD_DIM)
            ml = slice(hd * m, (hd + 1) * m)
            kh = k[i * m:(i + 1) * m, sl]
            vh = v[i * m:(i + 1) * m, sl]
            wqk = lax.dot_general(wq_ref[:, sl], kh, _NT, preferred_element_type=F32)
            wqk_ref[i, :, ml] = (wqk * (XATTN_HEAD_DIM ** -0.5)).astype(BF16)
            vwo_ref[i, ml, :] = _dot(vh, wo_ref[sl, :]).astype(BF16)


def _mem_weights(mem, nw, wk, wv, wq, wo):
    b, m, d = mem.shape
    hm = XATTN_HEADS * m
    const = _resident
    return pl.pallas_call(
        _mem_weights_kernel,
        out_shape=(jax.ShapeDtypeStruct((b, d, hm), BF16), jax.ShapeDtypeStruct((b, hm, d), BF16)),
        grid=(b // MEM_BATCH,),
        in_specs=[pl.BlockSpec((MEM_BATCH, m, d), lambda i: (i, 0, 0)), const(nw.shape), const(wk.shape),
                  const(wv.shape), const(wq.shape), const(wo.shape)],
        out_specs=(pl.BlockSpec((MEM_BATCH, d, hm), lambda i: (i, 0, 0)),
                   pl.BlockSpec((MEM_BATCH, hm, d), lambda i: (i, 0, 0))),
        compiler_params=pltpu.CompilerParams(
            dimension_semantics=("parallel",), vmem_limit_bytes=VMEM_LIMIT_BYTES),
        name="mem_weights",
    )(mem, nw, wk, wv, wq, wo)


def _xattn_kernel(x_ref, nw_ref, wqk_ref, vwo_ref, o_ref, *, ts, sub):
    n_mem = wqk_ref.shape[2] // XATTN_HEADS

    def scores(a):
        h = _rms(x_ref[0, a:a + sub, :], nw_ref[...]).astype(BF16)
        return _dot(h, wqk_ref[0])

    def attend(a, sc):
        parts = []
        for hd in range(XATTN_HEADS):
            s = sc[:, hd * n_mem:(hd + 1) * n_mem]
            e = jnp.exp(s - jnp.max(s, axis=-1, keepdims=True))
            parts.append((e * (1.0 / jnp.sum(e, axis=-1, keepdims=True))).astype(BF16))
        probs = jnp.concatenate(parts, axis=1)
        o_ref[0, a:a + sub, :] = x_ref[0, a:a + sub, :] + _dot(probs, vwo_ref[0])

    starts = list(range(0, ts, sub))
    sc = scores(starts[0])
    for i, a in enumerate(starts):
        nxt = scores(starts[i + 1]) if i + 1 < len(starts) else None
        attend(a, sc)
        sc = nxt


def _xattn(x, nw, wqk, vwo, *, ts):
    b, s, d = x.shape
    const = _resident
    return pl.pallas_call(
        functools.partial(_xattn_kernel, ts=ts, sub=min(ts, SUB_TILE)),
        out_shape=jax.ShapeDtypeStruct(x.shape, F32),
        grid=(b, s // ts),
        in_specs=[
            pl.BlockSpec((1, ts, d), lambda i, j: (i, j, 0)),
            const(nw.shape),
            pl.BlockSpec((1,) + wqk.shape[1:], lambda i, j: (i, 0, 0)),
            pl.BlockSpec((1,) + vwo.shape[1:], lambda i, j: (i, 0, 0)),
        ],
        out_specs=pl.BlockSpec((1, ts, d), lambda i, j: (i, j, 0)),
        compiler_params=pltpu.CompilerParams(
            dimension_semantics=("parallel", "parallel"), vmem_limit_bytes=VMEM_LIMIT_BYTES),
        name="xattn",
    )(x, nw, wqk, vwo)


def _ffn_kernel(x_ref, nw_ref, wup_ref, cw_ref, cb_ref, wdn_ref, nfw_ref, o_ref, ucarry, ubuf, act_s, *, ts):
    s = pl.program_id(1)

    @pl.when(s == 0)
    def _():
        ucarry[...] = jnp.zeros_like(ucarry)

    x = x_ref[0]
    h = _rms(x, nw_ref[...]).astype(BF16)

    def conv_cols(col0, slot):
        u = _dot(h, wup_ref[:, col0:col0 + FF_CHUNK])
        outs = []
        for k in range(FF_CHUNK // LANES):
            cols = slice(col0 + k * LANES, col0 + (k + 1) * LANES)
            uk = u[:, k * LANES:(k + 1) * LANES]
            buf = ubuf.at[slot, k]
            buf[:SUBLANES, :] = ucarry[:, cols]
            buf[SUBLANES:, :] = uk
            ucarry[:, cols] = uk[ts - SUBLANES:, :]
            cw = cw_ref[:, cols]
            outs.append(cb_ref[:, cols] + cw[0:1, :] * buf[SUBLANES - 2:SUBLANES - 2 + ts, :]
                        + cw[1:2, :] * buf[SUBLANES - 1:SUBLANES - 1 + ts, :] + cw[2:3, :] * uk)
        return jnp.concatenate(outs, axis=1)

    for j in range(D_FF // FF_CHUNK):
        gate = conv_cols(j * FF_CHUNK, (2 * j) % FF_UBUFS)
        val = conv_cols(D_FF + j * FF_CHUNK, (2 * j + 1) % FF_UBUFS)
        act_s[:, j * FF_CHUNK:(j + 1) * FF_CHUNK] = (gate * _sigmoid(gate) * val).astype(BF16)
    o_ref[0] = _rms(x + _dot(act_s[...], wdn_ref[...]), nfw_ref[...])


def _ffn(x, nw, wup, cw, cb, wdn, nfw, *, ts):
    b, s, d = x.shape
    const = _resident
    return pl.pallas_call(
        functools.partial(_ffn_kernel, ts=ts),
        out_shape=jax.ShapeDtypeStruct(x.shape, F32),
        grid=(b, s // ts),
        in_specs=[
            pl.BlockSpec((1, ts, d), lambda i, j: (i, j, 0)),
            const(nw.shape), const(wup.shape), const(cw.shape), const(cb.shape), const(wdn.shape),
            const(nfw.shape),
        ],
        out_specs=pl.BlockSpec((1, ts, d), lambda i, j: (i, j, 0)),
        scratch_shapes=[
            pltpu.VMEM((SUBLANES, 2 * D_FF), F32),
            pltpu.VMEM((FF_UBUFS, FF_CHUNK // LANES, SUBLANES + ts, LANES), F32),
            pltpu.VMEM((ts, D_FF), BF16),
        ],
        compiler_params=pltpu.CompilerParams(
            dimension_semantics=("parallel", "arbitrary"), vmem_limit_bytes=VMEM_LIMIT_BYTES),
        name="ffn",
    )(x, nw, wup, cw, cb, wdn, nfw)


def kernel(x, mem, norm_mix_w, w_in, pool_w, pool_scale, gk_w2, gk_b, gla_norm_w, w_out, norm_xattn_w,
           norm_mem_w, xattn_wq, xattn_wkv, xattn_wo, norm_ffn_w, ffn_w_up, ffn_conv_w, ffn_conv_b,
           ffn_w_down, norm_final_w):
    depth = w_in.shape[0]
    off_q = POOL_WIDTH
    off_v = off_q + 2 * GLA_KEY_WIDTH
    off_g = off_v + GLA_VALUE_WIDTH
    off_r = off_g + GLA_GATE_RANK
    row2 = lambda a: a.reshape(1, -1)
    for l in range(depth):
        wi = w_in[l]
        wg = jnp.pad(wi[:, off_g:off_r], ((0, 0), (0, LANES - GLA_GATE_RANK))).astype(BF16)
        gkw2 = jnp.pad(gk_w2[l], ((0, LANES - GLA_GATE_RANK), (0, 0))).astype(BF16)
        x = _mixer(
            x, row2(norm_mix_w[l]), wi[:, :off_q].astype(BF16), wi[:, off_q:off_v].astype(BF16),
            wi[:, off_v:off_g].astype(BF16), wg, wi[:, off_r:].astype(BF16),
            gkw2, row2(gk_b[l]), row2(gla_norm_w[l]), _fold_pool(pool_w[l], row2(pool_scale[l]), w_out[l]), ts=SEQ_TILE)
        wkv = xattn_wkv[l]
        wqk, vwo = _mem_weights(mem, row2(norm_mem_w[l]), wkv[:, :D_MODEL].astype(BF16), wkv[:, D_MODEL:].astype(BF16),
                                xattn_wq[l].astype(BF16), xattn_wo[l].astype(BF16))
        x = _xattn(x, row2(norm_xattn_w[l]), wqk, vwo, ts=XATTN_TILE)
        assert depth == 1
        x = _ffn(x, row2(norm_ffn_w[l]), ffn_w_up[l].astype(BF16), ffn_conv_w[l], row2(ffn_conv_b[l]),
                 ffn_w_down[l].astype(BF16), row2(norm_final_w), ts=SEQ_TILE)
    return x
```
